```python
import math
import jax, jax.numpy as jnp
from jax import lax
import numpy as np


D_MODEL = 1024
BATCH = 16
SEQ = 2048
DEPTH = 4

N_MIXERS = 3
N_A_LAYERS = (DEPTH + 2) // 3
N_B_LAYERS = (DEPTH + 1) // 3
N_C_LAYERS = DEPTH // 3
RMS_EPS = 1e-6

A_EXPAND = 128
A_HEADS = D_MODEL // A_EXPAND
A_HEAD_V = D_MODEL // A_HEADS
A_CHUNK = 64

B_HEAD_DIM = 128
B_HEADS = D_MODEL // B_HEAD_DIM
B_BLOCK = 256
B_TOPK = 3
B_QUERY_BLOCK = 128

REL_BUCKETS = 32
REL_MAX_DISTANCE = 1024

C_WINDOWS = (2, 4, 8, 16)
C_GROUPS = len(C_WINDOWS)
C_GROUP_DIM = D_MODEL // C_GROUPS

D_FF = 4 * D_MODEL

kernel_name = "hybrid_hgrn2_moba_pool_decoder"


def rmsnorm(x, gain):
    xf = x.astype(jnp.float32)
    y = xf * lax.rsqrt(jnp.mean(xf * xf, axis=-1, keepdims=True) + RMS_EPS)
    return (y * gain.astype(jnp.float32)).astype(x.dtype)


def gla_chunkwise(q, k, v, log_f):
    bsz, nh, s, dk = q.shape
    dv = v.shape[-1]
    n = s // A_CHUNK

    def to_chunks(t):
        return t.reshape(bsz, nh, n, A_CHUNK, t.shape[-1]).transpose(2, 0, 1, 3, 4)

    causal = jnp.tril(jnp.ones((A_CHUNK, A_CHUNK), bool))[:, :, None]

    def step(state, xs):
        qc, kc, vc, gc = xs
        b = jnp.cumsum(gc, axis=-2)
        o_inter = jnp.einsum('bhtd,bhde->bhte', qc * jnp.exp(b), state)
        diff = b[:, :, :, None, :] - b[:, :, None, :, :]
        decay = jnp.where(causal, jnp.exp(jnp.where(causal, diff, 0.0)), 0.0)
        attn = jnp.einsum('bhtd,bhsd,bhtsd->bhts', qc, kc, decay)
        o_intra = jnp.einsum('bhts,bhse->bhte', attn, vc)
        b_last = b[:, :, -1:, :]
        new_state = (jnp.exp(b_last[:, :, 0, :, None]) * state
                     + jnp.einsum('bhsd,bhse->bhde', kc * jnp.exp(b_last - b), vc))
        return new_state, o_inter + o_intra

    state0 = jnp.zeros((bsz, nh, dk, dv), jnp.float32)
    _, out = lax.scan(step, state0, (to_chunks(q), to_chunks(k), to_chunks(v), to_chunks(log_f)))
    return out.transpose(1, 2, 0, 3, 4).reshape(bsz, nh, s, dv)


def hgrn2_mixer(h, w_in, lower_bound, head_norm, w_out):
    bsz, s, _ = h.shape
    proj = h @ w_in
    q, f, i, g = jnp.split(proj, 4, axis=-1)

    def heads(t, d):
        return t.reshape(bsz, s, A_HEADS, d).transpose(0, 2, 1, 3).astype(jnp.float32)

    q = jax.nn.silu(heads(q, A_EXPAND))
    f = heads(f, A_EXPAND)
    v = heads(i, A_HEAD_V)
    lb = lower_bound.reshape(1, A_HEADS, 1, A_EXPAND)
    log_f = jnp.logaddexp(jnp.log(lb), jnp.log1p(-lb) + jax.nn.log_sigmoid(f))
    k = (1.0 - lb) * jax.nn.sigmoid(-f)
    o = gla_chunkwise(q, k, v, log_f)
    o = rmsnorm(o, head_norm)
    o = o.transpose(0, 2, 1, 3).reshape(bsz, s, D_MODEL).astype(h.dtype)
    return (o * jax.nn.silu(g)) @ w_out


def t5_bucket(dist):
    n = jnp.maximum(dist, 0)
    max_exact = REL_BUCKETS // 2
    nf = jnp.maximum(n, 1).astype(jnp.float32)
    large = max_exact + (jnp.log(nf / max_exact) / math.log(REL_MAX_DISTANCE / max_exact)
                         * (REL_BUCKETS - max_exact)).astype(jnp.int32)
    large = jnp.minimum(large, REL_BUCKETS - 1)
    return jnp.where(n < max_exact, n, large)


def moba_attention(q, k, v, rel_table):
    bsz, nh, s, dh = q.shape
    n_blk = -(-s // B_BLOCK)
    pad = n_blk * B_BLOCK - s
    kb = jnp.pad(k, ((0, 0), (0, 0), (0, pad), (0, 0))).reshape(bsz, nh, n_blk, B_BLOCK, dh)
    vb = jnp.pad(v, ((0, 0), (0, 0), (0, pad), (0, 0))).reshape(bsz, nh, n_blk, B_BLOCK, dh)
    k_mean = jnp.mean(kb, axis=3)
    gate = jnp.einsum('bhtd,bhnd->bhtn', q, k_mean)
    q_blk = jnp.arange(s) // B_BLOCK
    past = jnp.arange(n_blk)[None, :] < q_blk[:, None]
    gate = jnp.where(past, gate, -jnp.inf)
    n_sel = min(B_TOPK, n_blk)
    _, sel = lax.top_k(gate, n_sel)

    qb = B_QUERY_BLOCK
    n_q = s // qb

    def by_qblock(t):
        return t.reshape(bsz, nh, n_q, qb, t.shape[-1]).transpose(0, 2, 1, 3, 4).reshape(bsz * n_q, nh, qb, t.shape[-1])

    b_ids = jnp.repeat(jnp.arange(bsz), n_q)
    qb_ids = jnp.tile(jnp.arange(n_q), bsz)
    table_hb = rel_table.T.astype(jnp.float32)
    scale = 1.0 / math.sqrt(dh)
    h3 = jnp.arange(nh)[:, None, None]
    h4 = jnp.arange(nh)[:, None, None, None]
    key_off = jnp.arange(B_BLOCK)

    def attend(args):
        qq, sq, b_id, qb_id = args
        kb_b = kb[b_id]
        vb_b = vb[b_id]
        k_sel = kb_b[h3, sq]
        v_sel = vb_b[h3, sq]
        own = (qb_id * qb) // B_BLOCK
        k_own = lax.dynamic_index_in_dim(kb_b, own, axis=1, keepdims=False)
        v_own = lax.dynamic_index_in_dim(vb_b, own, axis=1, keepdims=False)
        t = qb_id * qb + jnp.arange(qb)
        s_sel = jnp.einsum('hqd,hqjkd->hqjk', qq, k_sel) * scale
        dist_sel = t[None, :, None, None] - (sq[..., None] * B_BLOCK + key_off)
        ok_sel = (jnp.arange(n_sel)[None, :, None] < (t // B_BLOCK)[:, None, None])[None]
        s_sel = jnp.where(ok_sel, s_sel + table_hb[h4, t5_bucket(dist_sel)], -jnp.inf)
        s_own = jnp.einsum('hqd,hkd->hqk', qq, k_own) * scale
        dist_own = t[:, None] - (own * B_BLOCK + key_off)[None, :]
        s_own = jnp.where((dist_own >= 0)[None], s_own + table_hb[h3, t5_bucket(dist_own)], -jnp.inf)
        scores = jnp.concatenate([s_sel.reshape(nh, qb, n_sel * B_BLOCK), s_own], axis=-1)
        p = jax.nn.softmax(scores, axis=-1)
        p_sel = p[..., :n_sel * B_BLOCK].reshape(nh, qb, n_sel, B_BLOCK)
        p_own = p[..., n_sel * B_BLOCK:]
        return (jnp.einsum('hqjk,hqjkd->hqd', p_sel, v_sel)
                + jnp.einsum('hqk,hkd->hqd', p_own, v_own))

    out = lax.map(attend, (by_qblock(q), by_qblock(sel), b_ids, qb_ids))
    return out.reshape(bsz, n_q, nh, qb, dh).transpose(0, 2, 1, 3, 4).reshape(bsz, nh, s, dh)


def moba_mixer(h, w_qkv, w_out, rel_table):
    bsz, s, _ = h.shape
    q, k, v = jnp.split(h @ w_qkv, 3, axis=-1)

    def heads(t):
        return t.reshape(bsz, s, B_HEADS, B_HEAD_DIM).transpose(0, 2, 1, 3).astype(jnp.float32)

    o = moba_attention(heads(q), heads(k), heads(v), rel_table)
    o = o.transpose(0, 2, 1, 3).reshape(bsz, s, D_MODEL).astype(h.dtype)
    return o @ w_out


def pool_mixer(h, w_group, scale):
    bsz, s, _ = h.shape
    hg = h.astype(jnp.float32).reshape(bsz, s, C_GROUPS, C_GROUP_DIM)
    cs = jnp.cumsum(hg, axis=1)
    pos1 = jnp.arange(1, s + 1).astype(jnp.float32)
    pooled = []
    for g, w in enumerate(C_WINDOWS):
        cg = cs[:, :, g]
        shifted = jnp.pad(cg, ((0, 0), (w, 0), (0, 0)))[:, :s]
        count = jnp.minimum(pos1, float(w))[None, :, None]
        pooled.append((cg - shifted) / count)
    pooled = jnp.stack(pooled, axis=2)
    mixed = jnp.einsum('bsgc,gcd->bsgd', pooled - hg, w_group.astype(jnp.float32))
    return (mixed.reshape(bsz, s, D_MODEL) * scale.astype(jnp.float32)).astype(h.dtype)


def sqrelu_mlp(h, w1, w2):
    a = jax.nn.relu(h @ w1)
    return (a * a) @ w2


def setup_inputs(seed: int = 0) -> dict:
    key = jax.random.key(seed)
    ks = jax.random.split(key, 16)

    def nrm(k, shape, fan_in):
        return jax.random.normal(k, shape, jnp.float32) * (fan_in ** -0.5)

    def gain(k, shape, s=0.05):
        return 1.0 + s * jax.random.normal(k, shape, jnp.float32)

    return {
        'x': jax.random.normal(ks[0], (BATCH, SEQ, D_MODEL), jnp.float32),
        'norm_mix': gain(ks[1], (DEPTH, D_MODEL)),
        'norm_mlp': gain(ks[2], (DEPTH, D_MODEL)),
        'norm_final': gain(ks[3], (D_MODEL,)),
        'a_w_in': nrm(ks[4], (N_A_LAYERS, D_MODEL, 4 * D_MODEL), D_MODEL),
        'a_lower_bound': jax.random.normal(ks[5], (N_A_LAYERS, A_HEADS * A_EXPAND), jnp.float32),
        'a_head_norm': gain(ks[6], (N_A_LAYERS, A_HEAD_V)),
        'a_w_out': nrm(ks[7], (N_A_LAYERS, D_MODEL, D_MODEL), D_MODEL),
        'b_w_qkv': nrm(ks[8], (N_B_LAYERS, D_MODEL, 3 * D_MODEL), D_MODEL),
        'b_w_out': nrm(ks[9], (N_B_LAYERS, D_MODEL, D_MODEL), D_MODEL),
        'rel_bias': 0.3 * jax.random.normal(ks[10], (REL_BUCKETS, B_HEADS), jnp.float32),
        'c_w_group': nrm(ks[11], (N_C_LAYERS, C_GROUPS, C_GROUP_DIM, C_GROUP_DIM), C_GROUP_DIM),
        'c_scale': gain(ks[12], (N_C_LAYERS, D_MODEL), 0.1),
        'mlp_w1': nrm(ks[13], (DEPTH, D_MODEL, D_FF), D_MODEL),
        'mlp_w2': nrm(ks[14], (DEPTH, D_FF, D_MODEL), D_FF),
    }


def reference(x, norm_mix, norm_mlp, norm_final, a_w_in, a_lower_bound, a_head_norm,
              a_w_out, b_w_qkv, b_w_out, rel_bias, c_w_group, c_scale, mlp_w1, mlp_w2):
    lb_cum = jnp.cumsum(jax.nn.softmax(a_lower_bound.astype(jnp.float32), axis=0), axis=0)
    lower_bounds = lb_cum - lb_cum[0:1]
    for i in range(DEPTH):
        mixer = i % N_MIXERS
        j = i // N_MIXERS
        h = rmsnorm(x, norm_mix[i])
        if mixer == 0:
            y = hgrn2_mixer(h, a_w_in[j], lower_bounds[j], a_head_norm[j], a_w_out[j])
        elif mixer == 1:
            y = moba_mixer(h, b_w_qkv[j], b_w_out[j], rel_bias)
        else:
            y = pool_mixer(h, c_w_group[j], c_scale[j])
        x = x + y.astype(x.dtype)
        x = x + sqrelu_mlp(rmsnorm(x, norm_mlp[i]), mlp_w1[i], mlp_w2[i]).astype(x.dtype)
    return rmsnorm(x, norm_final)
```

```python
import functools
import math

import numpy as np
import jax
import jax.numpy as jnp
from jax import lax
from jax.experimental import pallas as pl
from jax.experimental.pallas import tpu as pltpu

F32 = jnp.float32
BF16 = jnp.bfloat16

RMS_EPS = 1e-6
N_MIXERS = 3

A_HEAD = 128
A_CHUNK = 64
A_DIAG = 8

B_HEAD = 128
B_BLOCK = 256
B_TOPK = 3
REL_BUCKETS = 32
REL_MAX_DISTANCE = 1024

C_WINDOWS = (2, 4, 8, 16)
C_HALO = 16

VMEM_LIMIT_BYTES = 56 * 1024 * 1024

TOKEN_TILE = 1024
FF_TILE = 512
POOL_TILE = 512


def _params(*semantics):
    return pltpu.CompilerParams(dimension_semantics=semantics,
                                vmem_limit_bytes=VMEM_LIMIT_BYTES)


def _rms(x, gain):
    ms = jnp.mean(x * x, axis=-1, keepdims=True)
    return x * lax.rsqrt(ms + RMS_EPS) * gain


def _sigmoid(x):
    return 1.0 / (1.0 + jnp.exp(-x))


def _dot(a, b):
    return jnp.dot(a, b, preferred_element_type=F32)


def _dot_nt(a, b):
    return lax.dot_general(a, b, (((1,), (1,)), ((), ())), preferred_element_type=F32)


def _dot_tn(a, b):
    return lax.dot_general(a, b, (((0,), (0,)), ((), ())), preferred_element_type=F32)


def _hgrn_lower_bound(lb_ref, layer):
    a = lb_ref[...]
    e = jnp.exp(a - jnp.max(a, axis=0, keepdims=True))
    sm = e / jnp.sum(e, axis=0, keepdims=True)
    cum = jnp.sum(sm[:layer + 1], axis=0, keepdims=True)
    return cum - sm[0:1]


def _proj_kernel(x_ref, g_ref, w_ref, *rest, hgrn_layer):
    if hgrn_layer is None:
        out_ref, h_scr = rest
    else:
        lb_ref, out_ref, k_ref, h_scr = rest
    j = pl.program_id(1)

    @pl.when(j == 0)
    def _():
        h_scr[...] = _rms(x_ref[...], g_ref[...]).astype(BF16)

    y = _dot(h_scr[...], w_ref[...])

    if hgrn_layer is None:
        out_ref[...] = y
        return

    @pl.when(j == 0)
    def _():
        out_ref[...] = y * _sigmoid(y)

    @pl.when(j == 1)
    def _():
        lb = _hgrn_lower_bound(lb_ref, hgrn_layer)
        log_sig = jnp.minimum(y, 0.0) - jnp.log1p(jnp.exp(-jnp.abs(y)))
        a = jnp.log(lb)
        c = jnp.log1p(-lb) + log_sig
        out_ref[...] = jnp.maximum(a, c) + jnp.log1p(jnp.exp(-jnp.abs(a - c)))
        k_ref[...] = (1.0 - lb) * _sigmoid(-y)

    @pl.when(j == 2)
    def _():
        out_ref[...] = y

    @pl.when(j == 3)
    def _():
        out_ref[...] = y * _sigmoid(y)


def _norm_proj(x2, gain, w, lower_bound=None, hgrn_layer=None):
    t, d = x2.shape
    n = w.shape[1]
    tm, tn = TOKEN_TILE, d
    assert t % tm == 0 and n % tn == 0
    in_specs = [
        pl.BlockSpec((tm, d), lambda i, j: (i, 0)),
        pl.BlockSpec((1, d), lambda i, j: (0, 0)),
        pl.BlockSpec((d, tn), lambda i, j: (0, j)),
    ]
    args = [x2, gain.reshape(1, d), w]
    out_shape = [jax.ShapeDtypeStruct((t, n), F32)]
    out_specs = [pl.BlockSpec((tm, tn), lambda i, j: (i, j))]
    if hgrn_layer is not None:
        assert n == 4 * d
        in_specs.append(pl.BlockSpec(lower_bound.shape, lambda i, j: (0, 0)))
        args.append(lower_bound)
        out_shape.append(jax.ShapeDtypeStruct((t, d), F32))
        out_specs.append(pl.BlockSpec((tm, d), lambda i, j: (i, 0)))
    res = pl.pallas_call(
        functools.partial(_proj_kernel, hgrn_layer=hgrn_layer),
        grid=(t // tm, n // tn),
        in_specs=in_specs,
        out_specs=out_specs,
        out_shape=out_shape,
        scratch_shapes=[pltpu.VMEM((tm, d), BF16)],
        compiler_params=_params("parallel", "arbitrary"),
        name="norm_proj_hgrn" if hgrn_layer is not None else "norm_proj",
    )(*args)
    return res if hgrn_layer is not None else res[0]


def _out_proj_kernel(a_ref, w_ref, x_ref, o_ref):
    o_ref[...] = x_ref[...] + _dot(a_ref[...], w_ref[...])


def _out_proj(a2, w, x2):
    t, d = x2.shape
    tm = TOKEN_TILE
    return pl.pallas_call(
        _out_proj_kernel,
        grid=(t // tm,),
        in_specs=[
            pl.BlockSpec((tm, a2.shape[1]), lambda i: (i, 0)),
            pl.BlockSpec(w.shape, lambda i: (0, 0)),
            pl.BlockSpec((tm, d), lambda i: (i, 0)),
        ],
        out_specs=pl.BlockSpec((tm, d), lambda i: (i, 0)),
        out_shape=jax.ShapeDtypeStruct((t, d), F32),
        compiler_params=_params("parallel"),
        name="out_proj",
    )(a2, w, x2)


def _mlp_kernel(x_ref, g_ref, w1_ref, w2_ref, *rest, final_norm):
    if final_norm:
        gf_ref, o_ref, h_scr, acc_scr = rest
    else:
        o_ref, h_scr, acc_scr = rest
    j = pl.program_id(1)

    @pl.when(j == 0)
    def _():
        h_scr[...] = _rms(x_ref[...], g_ref[...]).astype(BF16)

    a = jnp.maximum(_dot(h_scr[...], w1_ref[...]), 0.0)
    part = _dot((a * a).astype(BF16), w2_ref[...])

    @pl.when(j == 0)
    def _():
        acc_scr[...] = part

    @pl.when(j > 0)
    def _():
        acc_scr[...] += part

    @pl.when(j == pl.num_programs(1) - 1)
    def _():
        y = x_ref[...] + acc_scr[...]
        o_ref[...] = _rms(y, gf_ref[...]) if final_norm else y


def _mlp(x2, gain, w1, w2, final_gain=None):
    t, d = x2.shape
    f = w1.shape[1]
    tm, tf = TOKEN_TILE, FF_TILE
    final_norm = final_gain is not None
    in_specs = [
        pl.BlockSpec((tm, d), lambda i, j: (i, 0)),
        pl.BlockSpec((1, d), lambda i, j: (0, 0)),
        pl.BlockSpec((d, tf), lambda i, j: (0, j)),
        pl.BlockSpec((tf, d), lambda i, j: (j, 0)),
    ]
    args = [x2, gain.reshape(1, d), w1, w2]
    if final_norm:
        in_specs.append(pl.BlockSpec((1, d), lambda i, j: (0, 0)))
        args.append(final_gain.reshape(1, d))
    return pl.pallas_call(
        functools.partial(_mlp_kernel, final_norm=final_norm),
        grid=(t // tm, f // tf),
        in_specs=in_specs,
        out_specs=pl.BlockSpec((tm, d), lambda i, j: (i, 0)),
        out_shape=jax.ShapeDtypeStruct((t, d), F32),
        scratch_shapes=[pltpu.VMEM((tm, d), BF16), pltpu.VMEM((tm, d), F32)],
        compiler_params=_params("parallel", "arbitrary"),
        name="mlp",
    )(*args)


def _prefix_sum_rows(x):
    n = x.shape[0]
    row = lax.broadcasted_iota(jnp.int32, x.shape, 0)
    sh = 1
    while sh < n:
        x = x + jnp.where(row >= sh, pltpu.roll(x, sh, 0), 0.0)
        sh *= 2
    return x


def _gla_kernel(qa_ref, lf_ref, k_ref, v_ref, sg_ref, hn_ref, o_ref, st_scr):
    c_len = A_CHUNK
    seq = qa_ref.shape[1]
    dk = qa_ref.shape[2]
    st_scr[...] = jnp.zeros_like(st_scr)

    row = lax.broadcasted_iota(jnp.int32, (c_len, c_len), 0)
    col = lax.broadcasted_iota(jnp.int32, (c_len, c_len), 1)
    rowd = lax.broadcasted_iota(jnp.int32, (c_len, dk), 0)

    def chunk(c, carry):
        r0 = pl.multiple_of(c * c_len, c_len)
        rows = pl.ds(r0, c_len)
        q = qa_ref[0, rows, :]
        k = k_ref[0, rows, :]
        v = v_ref[0, rows, :].astype(BF16)
        b = _prefix_sum_rows(lf_ref[0, rows, :])

        attn = jnp.zeros((c_len, c_len), F32)
        blk = c_len
        while blk > A_DIAG:
            half = blk // 2
            b_mid = jnp.concatenate(
                [jnp.broadcast_to(b[j * blk + half - 1:j * blk + half, :], (blk, dk))
                 for j in range(c_len // blk)], axis=0)
            second = (rowd % blk) >= half
            e = jnp.exp(jnp.where(second, b - b_mid, b_mid - b))
            qh = jnp.where(second, q * e, 0.0).astype(BF16)
            kh = jnp.where(second, 0.0, k * e).astype(BF16)
            s_l = _dot_nt(qh, kh)
            if blk < c_len:
                s_l = jnp.where((row // blk) == (col // blk), s_l, 0.0)
            attn = attn + s_l
            blk = half

        for delta in range(A_DIAG):
            if delta == 0:
                p = q * k
            else:
                k_d = pltpu.roll(k, delta, 0)
                b_d = pltpu.roll(b, delta, 0)
                p = q * k_d * jnp.exp(jnp.minimum(b - b_d, 0.0))
            r = jnp.sum(p, axis=1, keepdims=True)
            hit = jnp.logical_and(col == row - delta, (row % A_DIAG) >= delta)
            attn = jnp.where(hit, r, attn)

        st = st_scr[...]
        o = _dot(attn.astype(BF16), v) + _dot_nt((q * jnp.exp(b)).astype(BF16), st.astype(BF16))

        b_last = b[c_len - 1:c_len, :]
        k_dec = (k * jnp.exp(b_last - b)).astype(BF16)
        st_scr[...] = st * jnp.exp(b_last) + _dot_tn(v, k_dec)

        ms = jnp.mean(o * o, axis=-1, keepdims=True)
        o = o * lax.rsqrt(ms + RMS_EPS) * hn_ref[...]
        o_ref[0, rows, :] = (o * sg_ref[0, rows, :]).astype(o_ref.dtype)
        return carry

    lax.fori_loop(0, seq // c_len, chunk, 0)


def _gla(proj3, k3, head_norm):
    bsz, seq, d4 = proj3.shape
    d = d4 // 4
    nh = d // A_HEAD
    blk = (1, seq, A_HEAD)

    def sec(s):
        return pl.BlockSpec(blk, lambda b, h: (b, 0, s * nh + h))

    return pl.pallas_call(
        _gla_kernel,
        grid=(bsz, nh),
        in_specs=[sec(0), sec(1), pl.BlockSpec(blk, lambda b, h: (b, 0, h)), sec(2), sec(3),
                  pl.BlockSpec((1, A_HEAD), lambda b, h: (0, 0))],
        out_specs=pl.BlockSpec(blk, lambda b, h: (b, 0, h)),
        out_shape=jax.ShapeDtypeStruct((bsz, seq, d), BF16),
        scratch_shapes=[pltpu.VMEM((A_HEAD, A_HEAD), F32)],
        compiler_params=_params("parallel", "parallel"),
        name="gla",
    )(proj3, proj3, k3, proj3, proj3, head_norm.reshape(1, A_HEAD))


def _t5_large_thresholds():
    max_exact = REL_BUCKETS // 2
    n = np.arange(max_exact, 4 * REL_MAX_DISTANCE, dtype=np.float64)
    large = max_exact + (np.log(n / max_exact) / math.log(REL_MAX_DISTANCE / max_exact)
                         * (REL_BUCKETS - max_exact)).astype(np.int64)
    large = np.minimum(large, REL_BUCKETS - 1)
    assert np.all(np.diff(large) >= 0)
    return [int(n[np.argmax(large >= b)]) for b in range(max_exact + 1, REL_BUCKETS)]


def _moba_kernel(tab_ref, q_ref, k_ref, v_ref, o_ref):
    h = pl.program_id(1)
    seq, dh = q_ref.shape[1], q_ref.shape[2]
    nb = seq // B_BLOCK
    scale = 1.0 / math.sqrt(dh)
    q = q_ref[0]
    k = k_ref[0]
    q16 = q.astype(BF16)
    k16 = k.astype(BF16)
    v16 = v_ref[0].astype(BF16)

    k_mean = jnp.concatenate(
        [jnp.mean(k[n * B_BLOCK:(n + 1) * B_BLOCK], axis=0, keepdims=True) for n in range(nb)], axis=0)
    gate = lax.dot_general(k_mean, q, (((1,), (1,)), ((), ())),
                           precision=lax.Precision.HIGHEST, preferred_element_type=F32)
    blk_id = lax.broadcasted_iota(jnp.int32, (nb, seq), 0)
    tok_blk = lax.broadcasted_iota(jnp.int32, (nb, seq), 1) // B_BLOCK
    past = blk_id < tok_blk
    gate = jnp.where(past, gate, -jnp.inf)
    rank = jnp.zeros((nb, seq), jnp.int32)
    for m in range(nb):
        g_m = gate[m:m + 1, :]
        ahead = jnp.where(g_m > gate, 1, jnp.where(jnp.logical_and(g_m == gate, blk_id > m), 1, 0))
        rank = rank + ahead
    sel = jnp.where(past, jnp.where(rank < B_TOPK, 1.0, 0.0), 0.0)
    sel_t = jnp.concatenate([sel, jnp.zeros((128 - nb, seq), F32)], axis=0).T

    width = seq + B_BLOCK
    dist = seq - lax.broadcasted_iota(jnp.int32, (1, width), 1)
    n_pos = jnp.maximum(dist, 0)
    bucket = jnp.minimum(n_pos, REL_BUCKETS // 2)
    for thr in _t5_large_thresholds():
        bucket = bucket + jnp.where(n_pos >= thr, 1, 0)
    rvec = jnp.full((1, width), -jnp.inf, F32)
    for bkt in range(REL_BUCKETS):
        rvec = jnp.where(jnp.logical_and(bucket == bkt, dist >= 0), tab_ref[h, bkt], rvec)

    for qt in range(nb):
        w_keys = B_BLOCK * (qt + 1)
        a0 = seq - B_BLOCK * qt
        rows = slice(qt * B_BLOCK, (qt + 1) * B_BLOCK)
        s = _dot_nt(q16[rows], k16[:w_keys]) * scale
        wrow = jnp.concatenate([rvec[:, a0:a0 + w_keys], rvec[:, a0 - B_BLOCK:a0]], axis=1)
        tile = pltpu.roll(jnp.broadcast_to(wrow, (B_BLOCK, w_keys + B_BLOCK)), 0, 1,
                          stride=1, stride_axis=0)
        s = s + tile[:, :w_keys]
        sel_q = sel_t[rows]
        pieces = [jnp.where(sel_q[:, n:n + 1] > 0.5, s[:, n * B_BLOCK:(n + 1) * B_BLOCK], -jnp.inf)
                  for n in range(qt)]
        pieces.append(s[:, qt * B_BLOCK:])
        s = jnp.concatenate(pieces, axis=1) if qt else pieces[0]
        m = jnp.max(s, axis=1, keepdims=True)
        p = jnp.exp(s - m)
        l = jnp.sum(p, axis=1, keepdims=True)
        o = _dot(p.astype(BF16), v16[:w_keys]) / l
        o_ref[0, rows, :] = o.astype(o_ref.dtype)


def _moba(qkv3, rel_bias):
    bsz, seq, d3 = qkv3.shape
    d = d3 // 3
    nh = d // B_HEAD
    assert seq % B_BLOCK == 0 and seq // B_BLOCK <= 128
    blk = (1, seq, B_HEAD)

    def sec(s):
        return pl.BlockSpec(blk, lambda b, h: (b, 0, s * nh + h))

    return pl.pallas_call(
        _moba_kernel,
        grid=(bsz, nh),
        in_specs=[pl.BlockSpec(memory_space=pltpu.SMEM), sec(0), sec(1), sec(2)],
        out_specs=pl.BlockSpec(blk, lambda b, h: (b, 0, h)),
        out_shape=jax.ShapeDtypeStruct((bsz, seq, d), BF16),
        compiler_params=_params("parallel", "parallel"),
        name="moba",
    )(rel_bias.T.astype(F32), qkv3, qkv3, qkv3)


def _pool_kernel(x_ref, xp_ref, g_ref, w_ref, sc_ref, o_ref):
    i = pl.program_id(1)
    ts = x_ref.shape[1]
    x = x_ref[0]
    gain = g_ref[...]
    h = _rms(x, gain)
    h_prev = jnp.where(i > 0, _rms(xp_ref[0], gain), 0.0)
    hc = jnp.concatenate([h_prev, h], axis=0)
    pos1 = (i * ts + 1 + lax.broadcasted_iota(jnp.int32, (ts, 1), 0)).astype(F32)
    cg = x.shape[1] // len(C_WINDOWS)
    outs = []
    for g, w in enumerate(C_WINDOWS):
        cols = slice(g * cg, (g + 1) * cg)
        acc = hc[:, cols]
        span = 1
        while span < w:
            acc = acc + pltpu.roll(acc, span, 0)
            span *= 2
        pooled = acc[C_HALO:] / jnp.minimum(pos1, float(w))
        outs.append(_dot((pooled - h[:, cols]).astype(BF16), w_ref[g]))
    y = jnp.concatenate(outs, axis=1) * sc_ref[...]
    o_ref[0] = x + y


def _pool(x3, gain, w_group, scale):
    bsz, seq, d = x3.shape
    ts = POOL_TILE
    assert seq % ts == 0 and ts % C_HALO == 0 and max(C_WINDOWS) <= C_HALO
    per = ts // C_HALO
    return pl.pallas_call(
        _pool_kernel,
        grid=(bsz, seq // ts),
        in_specs=[
            pl.BlockSpec((1, ts, d), lambda b, i: (b, i, 0)),
            pl.BlockSpec((1, C_HALO, d), lambda b, i: (b, jnp.maximum(i * per - 1, 0), 0)),
            pl.BlockSpec((1, d), lambda b, i: (0, 0)),
            pl.BlockSpec(w_group.shape, lambda b, i: (0, 0, 0)),
            pl.BlockSpec((1, d), lambda b, i: (0, 0)),
        ],
        out_specs=pl.BlockSpec((1, ts, d), lambda b, i: (b, i, 0)),
        out_shape=jax.ShapeDtypeStruct((bsz, seq, d), F32),
        compiler_params=_params("parallel", "parallel"),
        name="pool",
    )(x3, x3, gain.reshape(1, d), w_group, scale.reshape(1, d))


def kernel(x, norm_mix, norm_mlp, norm_final, a_w_in, a_lower_bound, a_head_norm, a_w_out,
           b_w_qkv, b_w_out, rel_bias, c_w_group, c_scale, mlp_w1, mlp_w2):
    bsz, seq, d = x.shape
    depth = norm_mix.shape[0]
    t = bsz * seq
    x2 = x.reshape(t, d)
    for i in range(depth):
        mixer, j = i % N_MIXERS, i // N_MIXERS
        if mixer == 0:
            proj, kk = _norm_proj(x2, norm_mix[i], a_w_in[j].astype(BF16),
                                  lower_bound=a_lower_bound.astype(F32), hgrn_layer=j)
            og = _gla(proj.reshape(bsz, seq, 4 * d), kk.reshape(bsz, seq, d), a_head_norm[j])
            x2 = _out_proj(og.reshape(t, d), a_w_out[j].astype(BF16), x2)
        elif mixer == 1:
            qkv = _norm_proj(x2, norm_mix[i], b_w_qkv[j].astype(BF16))
            o = _moba(qkv.reshape(bsz, seq, 3 * d), rel_bias)
            x2 = _out_proj(o.reshape(t, d), b_w_out[j].astype(BF16), x2)
        else:
            x2 = _pool(x2.reshape(bsz, seq, d), norm_mix[i], c_w_group[j].astype(BF16),
                       c_scale[j]).reshape(t, d)
        x2 = _mlp(x2, norm_mlp[i], mlp_w1[i].astype(BF16), mlp_w2[i].astype(BF16),
                  final_gain=norm_final if i == depth - 1 else None)
    return x2.reshape(bsz, seq, d)
```

```python
import functools
import math

import numpy as np
import jax
import jax.numpy as jnp
from jax import lax
from jax.experimental import pallas as pl
from jax.experimental.pallas import tpu as pltpu

F32 = jnp.float32
BF16 = jnp.bfloat16

RMS_EPS = 1e-6
N_MIXERS = 3

A_HEAD = 128
A_CHUNK = 64
A_GROUP = 8
LOG2E = 1.4426950408889634

B_HEAD = 128
B_BLOCK = 256
B_TOPK = 3
REL_BUCKETS = 32
REL_MAX_DISTANCE = 1024

C_WINDOWS = (2, 4, 8, 16)
C_HALO = 16

VMEM_LIMIT_BYTES = 56 * 1024 * 1024

TOKEN_TILE = 1024
FF_TILE = 512
POOL_TILE = 512


def _params(*semantics):
    return pltpu.CompilerParams(dimension_semantics=semantics,
                                vmem_limit_bytes=VMEM_LIMIT_BYTES)


def _rms(x, gain):
    ms = jnp.mean(x * x, axis=-1, keepdims=True)
    return x * lax.rsqrt(ms + RMS_EPS) * gain


def _sigmoid(x):
    return 1.0 / (1.0 + jnp.exp(-x))


def _dot(a, b):
    return jnp.dot(a, b, preferred_element_type=F32)


def _dot_nt(a, b):
    return lax.dot_general(a, b, (((1,), (1,)), ((), ())), preferred_element_type=F32)


def _dot_tn(a, b):
    return lax.dot_general(a, b, (((0,), (0,)), ((), ())), preferred_element_type=F32)


def _hgrn_lower_bound(lb_ref, layer):
    a = lb_ref[...]
    e = jnp.exp(a - jnp.max(a, axis=0, keepdims=True))
    sm = e / jnp.sum(e, axis=0, keepdims=True)
    cum = jnp.sum(sm[:layer + 1], axis=0, keepdims=True)
    return cum - sm[0:1]


def _proj_kernel(x_ref, g_ref, w_ref, *rest, hgrn_layer):
    if hgrn_layer is None:
        out_ref, h_scr = rest
    else:
        lb_ref, out_ref, k_ref, h_scr = rest
    j = pl.program_id(1)

    @pl.when(j == 0)
    def _():
        h_scr[...] = _rms(x_ref[...], g_ref[...]).astype(BF16)

    y = _dot(h_scr[...], w_ref[...])

    if hgrn_layer is None:
        out_ref[...] = y
        return

    @pl.when(j == 0)
    def _():
        out_ref[...] = y * _sigmoid(y)

    @pl.when(j == 1)
    def _():
        lb = _hgrn_lower_bound(lb_ref, hgrn_layer)
        log_sig = jnp.minimum(y, 0.0) - jnp.log1p(jnp.exp(-jnp.abs(y)))
        a = jnp.log(lb)
        c = jnp.log1p(-lb) + log_sig
        out_ref[...] = jnp.maximum(a, c) + jnp.log1p(jnp.exp(-jnp.abs(a - c)))
        k_ref[...] = (1.0 - lb) * _sigmoid(-y)

    @pl.when(j == 2)
    def _():
        out_ref[...] = y

    @pl.when(j == 3)
    def _():
        out_ref[...] = y * _sigmoid(y)


def _norm_proj(x2, gain, w, lower_bound=None, hgrn_layer=None):
    t, d = x2.shape
    n = w.shape[1]
    tm, tn = TOKEN_TILE, d
    assert t % tm == 0 and n % tn == 0
    in_specs = [
        pl.BlockSpec((tm, d), lambda i, j: (i, 0)),
        pl.BlockSpec((1, d), lambda i, j: (0, 0)),
        pl.BlockSpec((d, tn), lambda i, j: (0, j)),
    ]
    args = [x2, gain.reshape(1, d), w]
    out_shape = [jax.ShapeDtypeStruct((t, n), F32)]
    out_specs = [pl.BlockSpec((tm, tn), lambda i, j: (i, j))]
    if hgrn_layer is not None:
        assert n == 4 * d
        in_specs.append(pl.BlockSpec(lower_bound.shape, lambda i, j: (0, 0)))
        args.append(lower_bound)
        out_shape.append(jax.ShapeDtypeStruct((t, d), F32))
        out_specs.append(pl.BlockSpec((tm, d), lambda i, j: (i, 0)))
    res = pl.pallas_call(
        functools.partial(_proj_kernel, hgrn_layer=hgrn_layer),
        grid=(t // tm, n // tn),
        in_specs=in_specs,
        out_specs=out_specs,
        out_shape=out_shape,
        scratch_shapes=[pltpu.VMEM((tm, d), BF16)],
        compiler_params=_params("parallel", "arbitrary"),
        name="norm_proj_hgrn" if hgrn_layer is not None else "norm_proj",
    )(*args)
    return res if hgrn_layer is not None else res[0]


def _out_proj_kernel(a_ref, w_ref, x_ref, o_ref):
    o_ref[...] = x_ref[...] + _dot(a_ref[...], w_ref[...])


def _out_proj(a2, w, x2):
    t, d = x2.shape
    tm = TOKEN_TILE
    return pl.pallas_call(
        _out_proj_kernel,
        grid=(t // tm,),
        in_specs=[
            pl.BlockSpec((tm, a2.shape[1]), lambda i: (i, 0)),
            pl.BlockSpec(w.shape, lambda i: (0, 0)),
            pl.BlockSpec((tm, d), lambda i: (i, 0)),
        ],
        out_specs=pl.BlockSpec((tm, d), lambda i: (i, 0)),
        out_shape=jax.ShapeDtypeStruct((t, d), F32),
        compiler_params=_params("parallel"),
        name="out_proj",
    )(a2, w, x2)


def _mlp_kernel(x_ref, g_ref, w1_ref, w2_ref, *rest, final_norm):
    if final_norm:
        gf_ref, o_ref, h_scr, acc_scr = rest
    else:
        o_ref, h_scr, acc_scr = rest
    j = pl.program_id(1)

    @pl.when(j == 0)
    def _():
        h_scr[...] = _rms(x_ref[...], g_ref[...]).astype(BF16)

    a = jnp.maximum(_dot(h_scr[...], w1_ref[...]), 0.0)
    part = _dot((a * a).astype(BF16), w2_ref[...])

    @pl.when(j == 0)
    def _():
        acc_scr[...] = part

    @pl.when(j > 0)
    def _():
        acc_scr[...] += part

    @pl.when(j == pl.num_programs(1) - 1)
    def _():
        y = x_ref[...] + acc_scr[...]
        o_ref[...] = _rms(y, gf_ref[...]) if final_norm else y


def _mlp(x2, gain, w1, w2, final_gain=None):
    t, d = x2.shape
    f = w1.shape[1]
    tm, tf = TOKEN_TILE, FF_TILE
    final_norm = final_gain is not None
    in_specs = [
        pl.BlockSpec((tm, d), lambda i, j: (i, 0)),
        pl.BlockSpec((1, d), lambda i, j: (0, 0)),
        pl.BlockSpec((d, tf), lambda i, j: (0, j)),
        pl.BlockSpec((tf, d), lambda i, j: (j, 0)),
    ]
    args = [x2, gain.reshape(1, d), w1, w2]
    if final_norm:
        in_specs.append(pl.BlockSpec((1, d), lambda i, j: (0, 0)))
        args.append(final_gain.reshape(1, d))
    return pl.pallas_call(
        functools.partial(_mlp_kernel, final_norm=final_norm),
        grid=(t // tm, f // tf),
        in_specs=in_specs,
        out_specs=pl.BlockSpec((tm, d), lambda i, j: (i, 0)),
        out_shape=jax.ShapeDtypeStruct((t, d), F32),
        scratch_shapes=[pltpu.VMEM((tm, d), BF16), pltpu.VMEM((tm, d), F32)],
        compiler_params=_params("parallel", "arbitrary"),
        name="mlp",
    )(*args)


def _gla_levels():
    out, blk = [], A_CHUNK
    while blk >= 2:
        out.append(blk)
        blk //= 2
    return out


def _gla_constants():
    t = np.arange(A_CHUNK)
    sgn, msk = [], []
    for blk in _gla_levels():
        second = (t % blk) >= blk // 2
        sgn.append(np.broadcast_to(np.where(second, 1.0, -1.0)[:, None], (A_CHUNK, A_HEAD)))
        same = (t[:, None] // blk) == (t[None, :] // blk)
        msk.append(same & second[:, None] & (~second)[None, :])
    msk.append(np.eye(A_CHUNK, dtype=bool))
    return jnp.asarray(np.stack(sgn), F32), jnp.asarray(np.stack(msk), F32)


def _bdot_nt(a, b):
    return lax.dot_general(a, b, (((2,), (2,)), ((0,), (0,))), preferred_element_type=F32)


def _gla_kernel(qa_ref, lf_ref, k_ref, v_ref, sg_ref, hn_ref, sgn_ref, msk_ref, o_ref, st_scr, b_scr):
    c_len, grp = A_CHUNK, A_GROUP
    g_len = grp * c_len
    nv = c_len // 8
    seq, dk = qa_ref.shape[1], qa_ref.shape[2]
    levels = _gla_levels()
    sub = lax.broadcasted_iota(jnp.int32, (grp * nv, 8, dk), 1)
    sub8 = lax.broadcasted_iota(jnp.int32, (8, dk), 0)
    st_scr[...] = jnp.zeros_like(st_scr)

    def group(gi, carry):
        r0 = pl.multiple_of(gi * g_len, g_len)
        rows = pl.ds(r0, g_len)
        q = qa_ref[0, rows, :]
        k = k_ref[0, rows, :]
        v3 = v_ref[0, rows, :].astype(BF16).reshape(grp, c_len, dk)

        x3 = (lf_ref[0, rows, :] * LOG2E).reshape(grp * nv, 8, dk)
        for sh in (1, 2, 4):
            x3 = x3 + jnp.where(sub >= sh, pltpu.roll(x3, sh, 1), 0.0)
        tot = x3[:, 7:8, :]
        pieces = []
        for g in range(grp):
            run = None
            for j in range(nv):
                i = g * nv + j
                pieces.append(x3[i] if run is None else x3[i] + run)
                run = tot[i] if run is None else run + tot[i]
        b2 = jnp.concatenate(pieces, axis=0)
        b_scr[rows, :] = b2

        def row_b(r, n):
            return jnp.broadcast_to(b_scr[pl.ds(r0 + r, 1), :], (n, dk))

        q3 = q.reshape(grp, c_len, dk)
        k3 = k.reshape(grp, c_len, dk)
        b3 = b2.reshape(grp, c_len, dk)

        attn = _bdot_nt(q3.astype(BF16), k3.astype(BF16)) * msk_ref[len(levels)][None]
        for li, blk in enumerate(levels):
            half = blk // 2
            parts = []
            for g in range(grp):
                for j in range(nv):
                    base = g * c_len + ((8 * j // blk) * blk if blk >= 8 else 8 * j)
                    val = row_b(base + half - 1, 8)
                    for s_i in range(1, 8 // blk):
                        val = jnp.where(sub8 >= s_i * blk, row_b(base + s_i * blk + half - 1, 8), val)
                    parts.append(val)
            b_mid = jnp.concatenate(parts, axis=0).reshape(grp, c_len, dk)
            sg = sgn_ref[li][None]
            x_l = (jnp.where(sg > 0.0, q3, k3) * jnp.exp2((b3 - b_mid) * sg)).astype(BF16)
            attn = attn + _bdot_nt(x_l, x_l) * msk_ref[li][None]

        o_in = lax.dot_general(attn.astype(BF16), v3, (((2,), (1,)), ((0,), (0,))),
                               preferred_element_type=F32)
        b_last = jnp.concatenate([row_b(g * c_len + c_len - 1, c_len) for g in range(grp)], axis=0)
        k_dec = (k * jnp.exp2(b_last - b2)).astype(BF16).reshape(grp, c_len, dk)
        upd = lax.dot_general(v3, k_dec, (((1,), (1,)), ((0,), (0,))), preferred_element_type=F32)
        qe = (q * jnp.exp2(b2)).astype(BF16).reshape(grp, c_len, dk)

        st = st_scr[...]
        sts = []
        for g in range(grp):
            sts.append(st.astype(BF16))
            st = st * jnp.exp2(row_b(g * c_len + c_len - 1, 1)) + upd[g]
        st_scr[...] = st
        o = (o_in + _bdot_nt(qe, jnp.stack(sts, axis=0))).reshape(g_len, dk)
        ms = jnp.mean(o * o, axis=-1, keepdims=True)
        o = o * lax.rsqrt(ms + RMS_EPS) * hn_ref[...]
        o_ref[0, rows, :] = (o * sg_ref[0, rows, :]).astype(o_ref.dtype)
        return carry

    lax.fori_loop(0, seq // g_len, group, 0)


def _gla(proj3, k3, head_norm):
    bsz, seq, d4 = proj3.shape
    d = d4 // 4
    nh = d // A_HEAD
    assert seq % (A_GROUP * A_CHUNK) == 0
    blk = (1, seq, A_HEAD)
    sgn, msk = _gla_constants()

    def sec(s):
        return pl.BlockSpec(blk, lambda b, h: (b, 0, s * nh + h))

    return pl.pallas_call(
        _gla_kernel,
        grid=(bsz, nh),
        in_specs=[sec(0), sec(1), pl.BlockSpec(blk, lambda b, h: (b, 0, h)), sec(2), sec(3),
                  pl.BlockSpec((1, A_HEAD), lambda b, h: (0, 0)),
                  pl.BlockSpec(sgn.shape, lambda b, h: (0, 0, 0)),
                  pl.BlockSpec(msk.shape, lambda b, h: (0, 0, 0))],
        out_specs=pl.BlockSpec(blk, lambda b, h: (b, 0, h)),
        out_shape=jax.ShapeDtypeStruct((bsz, seq, d), BF16),
        scratch_shapes=[pltpu.VMEM((A_HEAD, A_HEAD), F32), pltpu.VMEM((seq, A_HEAD), F32)],
        compiler_params=_params("parallel", "parallel"),
        name="gla",
    )(proj3, proj3, k3, proj3, proj3, head_norm.reshape(1, A_HEAD), sgn, msk)


def _t5_large_thresholds():
    max_exact = REL_BUCKETS // 2
    n = np.arange(max_exact, 4 * REL_MAX_DISTANCE, dtype=np.float64)
    large = max_exact + (np.log(n / max_exact) / math.log(REL_MAX_DISTANCE / max_exact)
                         * (REL_BUCKETS - max_exact)).astype(np.int64)
    large = np.minimum(large, REL_BUCKETS - 1)
    assert np.all(np.diff(large) >= 0)
    return [int(n[np.argmax(large >= b)]) for b in range(max_exact + 1, REL_BUCKETS)]


def _moba_kernel(tab_ref, q_ref, k_ref, v_ref, o_ref):
    h = pl.program_id(1)
    seq, dh = q_ref.shape[1], q_ref.shape[2]
    nb = seq // B_BLOCK
    scale = 1.0 / math.sqrt(dh)
    q = q_ref[0]
    k = k_ref[0]
    q16 = q.astype(BF16)
    k16 = k.astype(BF16)
    v16 = v_ref[0].astype(BF16)

    k_mean = jnp.concatenate(
        [jnp.mean(k[n * B_BLOCK:(n + 1) * B_BLOCK], axis=0, keepdims=True) for n in range(nb)], axis=0)
    gate = lax.dot_general(k_mean, q, (((1,), (1,)), ((), ())),
                           precision=lax.Precision.HIGHEST, preferred_element_type=F32)
    blk_id = lax.broadcasted_iota(jnp.int32, (nb, seq), 0)
    tok_blk = lax.broadcasted_iota(jnp.int32, (nb, seq), 1) // B_BLOCK
    past = blk_id < tok_blk
    gate = jnp.where(past, gate, -jnp.inf)
    rank = jnp.zeros((nb, seq), jnp.int32)
    for m in range(nb):
        g_m = gate[m:m + 1, :]
        ahead = jnp.where(g_m > gate, 1, jnp.where(jnp.logical_and(g_m == gate, blk_id > m), 1, 0))
        rank = rank + ahead
    sel = jnp.where(past, jnp.where(rank < B_TOPK, 1.0, 0.0), 0.0)
    sel_t = jnp.concatenate([sel, jnp.zeros((128 - nb, seq), F32)], axis=0).T

    width = seq + B_BLOCK
    dist = seq - lax.broadcasted_iota(jnp.int32, (1, width), 1)
    n_pos = jnp.maximum(dist, 0)
    bucket = jnp.minimum(n_pos, REL_BUCKETS // 2)
    for thr in _t5_large_thresholds():
        bucket = bucket + jnp.where(n_pos >= thr, 1, 0)
    rvec = jnp.full((1, width), -jnp.inf, F32)
    for bkt in range(REL_BUCKETS):
        rvec = jnp.where(jnp.logical_and(bucket == bkt, dist >= 0), tab_ref[h, bkt], rvec)

    for qt in range(nb):
        w_keys = B_BLOCK * (qt + 1)
        a0 = seq - B_BLOCK * qt
        rows = slice(qt * B_BLOCK, (qt + 1) * B_BLOCK)
        s = _dot_nt(q16[rows], k16[:w_keys]) * scale
        wrow = jnp.concatenate([rvec[:, a0:a0 + w_keys], rvec[:, a0 - B_BLOCK:a0]], axis=1)
        tile = pltpu.roll(jnp.broadcast_to(wrow, (B_BLOCK, w_keys + B_BLOCK)), 0, 1,
                          stride=1, stride_axis=0)
        s = s + tile[:, :w_keys]
        sel_q = sel_t[rows]
        pieces = [jnp.where(sel_q[:, n:n + 1] > 0.5, s[:, n * B_BLOCK:(n + 1) * B_BLOCK], -jnp.inf)
                  for n in range(qt)]
        pieces.append(s[:, qt * B_BLOCK:])
        s = jnp.concatenate(pieces, axis=1) if qt else pieces[0]
        m = jnp.max(s, axis=1, keepdims=True)
        p = jnp.exp(s - m)
        l = jnp.sum(p, axis=1, keepdims=True)
        o = _dot(p.astype(BF16), v16[:w_keys]) / l
        o_ref[0, rows, :] = o.astype(o_ref.dtype)


def _moba(qkv3, rel_bias):
    bsz, seq, d3 = qkv3.shape
    d = d3 // 3
    nh = d // B_HEAD
    assert seq % B_BLOCK == 0 and seq // B_BLOCK <= 128
    blk = (1, seq, B_HEAD)

    def sec(s):
        return pl.BlockSpec(blk, lambda b, h: (b, 0, s * nh + h))

    return pl.pallas_call(
        _moba_kernel,
        grid=(bsz, nh),
        in_specs=[pl.BlockSpec(memory_space=pltpu.SMEM), sec(0), sec(1), sec(2)],
        out_specs=pl.BlockSpec(blk, lambda b, h: (b, 0, h)),
        out_shape=jax.ShapeDtypeStruct((bsz, seq, d), BF16),
        compiler_params=_params("parallel", "parallel"),
        name="moba",
    )(rel_bias.T.astype(F32), qkv3, qkv3, qkv3)


def _pool_kernel(x_ref, xp_ref, g_ref, w_ref, sc_ref, o_ref):
    i = pl.program_id(1)
    ts = x_ref.shape[1]
    x = x_ref[0]
    gain = g_ref[...]
    h = _rms(x, gain)
    h_prev = jnp.where(i > 0, _rms(xp_ref[0], gain), 0.0)
    hc = jnp.concatenate([h_prev, h], axis=0)
    pos1 = (i * ts + 1 + lax.broadcasted_iota(jnp.int32, (ts, 1), 0)).astype(F32)
    cg = x.shape[1] // len(C_WINDOWS)
    outs = []
    for g, w in enumerate(C_WINDOWS):
        cols = slice(g * cg, (g + 1) * cg)
        acc = hc[:, cols]
        span = 1
        while span < w:
            acc = acc + pltpu.roll(acc, span, 0)
            span *= 2
        pooled = acc[C_HALO:] / jnp.minimum(pos1, float(w))
        outs.append(_dot((pooled - h[:, cols]).astype(BF16), w_ref[g]))
    y = jnp.concatenate(outs, axis=1) * sc_ref[...]
    o_ref[0] = x + y


def _pool(x3, gain, w_group, scale):
    bsz, seq, d = x3.shape
    ts = POOL_TILE
    assert seq % ts == 0 and ts % C_HALO == 0 and max(C_WINDOWS) <= C_HALO
    per = ts // C_HALO
    return pl.pallas_call(
        _pool_kernel,
        grid=(bsz, seq // ts),
        in_specs=[
            pl.BlockSpec((1, ts, d), lambda b, i: (b, i, 0)),
            pl.BlockSpec((1, C_HALO, d), lambda b, i: (b, jnp.maximum(i * per - 1, 0), 0)),
            pl.BlockSpec((1, d), lambda b, i: (0, 0)),
            pl.BlockSpec(w_group.shape, lambda b, i: (0, 0, 0)),
            pl.BlockSpec((1, d), lambda b, i: (0, 0)),
        ],
        out_specs=pl.BlockSpec((1, ts, d), lambda b, i: (b, i, 0)),
        out_shape=jax.ShapeDtypeStruct((bsz, seq, d), F32),
        compiler_params=_params("parallel", "parallel"),
        name="pool",
    )(x3, x3, gain.reshape(1, d), w_group, scale.reshape(1, d))


def kernel(x, norm_mix, norm_mlp, norm_final, a_w_in, a_lower_bound, a_head_norm, a_w_out,
           b_w_qkv, b_w_out, rel_bias, c_w_group, c_scale, mlp_w1, mlp_w2):
    bsz, seq, d = x.shape
    depth = norm_mix.shape[0]
    t = bsz * seq
    x2 = x.reshape(t, d)
    for i in range(depth):
        mixer, j = i % N_MIXERS, i // N_MIXERS
        if mixer == 0:
            proj, kk = _norm_proj(x2, norm_mix[i], a_w_in[j].astype(BF16),
                                  lower_bound=a_lower_bound.astype(F32), hgrn_layer=j)
            og = _gla(proj.reshape(bsz, seq, 4 * d), kk.reshape(bsz, seq, d), a_head_norm[j])
            x2 = _out_proj(og.reshape(t, d), a_w_out[j].astype(BF16), x2)
        elif mixer == 1:
            qkv = _norm_proj(x2, norm_mix[i], b_w_qkv[j].astype(BF16))
            o = _moba(qkv.reshape(bsz, seq, 3 * d), rel_bias)
            x2 = _out_proj(o.reshape(t, d), b_w_out[j].astype(BF16), x2)
        else:
            x2 = _pool(x2.reshape(bsz, seq, d), norm_mix[i], c_w_group[j].astype(BF16),
                       c_scale[j]).reshape(t, d)
        x2 = _mlp(x2, norm_mlp[i], mlp_w1[i].astype(BF16), mlp_w2[i].astype(BF16),
                  final_gain=norm_final if i == depth - 1 else None)
    return x2.reshape(bsz, seq, d)
```

```python
import functools
import math

import numpy as np
import jax
import jax.numpy as jnp
from jax import lax
from jax.experimental import pallas as pl
from jax.experimental.pallas import tpu as pltpu

F32 = jnp.float32
BF16 = jnp.bfloat16

RMS_EPS = 1e-6
N_MIXERS = 3

A_HEAD = 128
A_CHUNK = 64
A_GROUP = 8
LOG2E = 1.4426950408889634

B_HEAD = 128
B_BLOCK = 256
B_TOPK = 3
REL_BUCKETS = 32
REL_MAX_DISTANCE = 1024

C_WINDOWS = (2, 4, 8, 16)
C_HALO = 16

VMEM_LIMIT_BYTES = 56 * 1024 * 1024

TOKEN_TILE = 1024
PROJ_TILE = 512
FF_TILE = 1024
POOL_TILE = 512


def _params(*semantics):
    return pltpu.CompilerParams(dimension_semantics=semantics,
                                vmem_limit_bytes=VMEM_LIMIT_BYTES)


def _rms(x, gain):
    ms = jnp.mean(x * x, axis=-1, keepdims=True)
    return x * lax.rsqrt(ms + RMS_EPS) * gain


def _dot(a, b):
    return jnp.dot(a, b, preferred_element_type=F32)


def _dot_nt(a, b):
    return lax.dot_general(a, b, (((1,), (1,)), ((), ())), preferred_element_type=F32)


def _dot_tn(a, b):
    return lax.dot_general(a, b, (((0,), (0,)), ((), ())), preferred_element_type=F32)


def _hgrn_lower_bound(lb_ref, layer):
    a = lb_ref[...]
    e = jnp.exp(a - jnp.max(a, axis=0, keepdims=True))
    sm = e / jnp.sum(e, axis=0, keepdims=True)
    cum = jnp.sum(sm[:layer + 1], axis=0, keepdims=True)
    return cum - sm[0:1]


def _silu(y):
    return (0.5 * y) * (1.0 + jnp.tanh(0.5 * y))


def _proj_kernel(x_ref, g_ref, w_ref, *rest, hgrn_layer):
    d = x_ref.shape[1]
    h = _rms(x_ref[...], g_ref[...]).astype(BF16)

    def section(s):
        return _dot(h, w_ref[:, s * d:(s + 1) * d])

    if hgrn_layer is None:
        for s, o_ref in enumerate(rest):
            o_ref[...] = section(s).astype(o_ref.dtype)
        return

    lb_ref, qa_ref, lf_ref, k_ref, v_ref, sg_ref = rest
    y_q, y, y_v, y_g = [section(s) for s in range(4)]
    qa_ref[...] = _silu(y_q).astype(qa_ref.dtype)
    v_ref[...] = y_v.astype(v_ref.dtype)
    sg_ref[...] = _silu(y_g).astype(sg_ref.dtype)

    lb = _hgrn_lower_bound(lb_ref, hgrn_layer)
    a = jnp.log2(lb)
    u = jnp.exp2(jnp.abs(y) * (-LOG2E))
    w1 = 1.0 + u
    c = jnp.log2(1.0 - lb) + (jnp.minimum(y, 0.0) * LOG2E - jnp.log2(w1))
    lf_ref[...] = jnp.maximum(a, c) + jnp.log2(1.0 + jnp.exp2(-jnp.abs(a - c)))
    k_ref[...] = ((1.0 - lb) * jnp.where(y >= 0.0, u, 1.0) / w1).astype(k_ref.dtype)


def _norm_proj(x2, gain, w, lower_bound=None, hgrn_layer=None):
    t, d = x2.shape
    n = w.shape[1]
    tm = PROJ_TILE
    assert t % tm == 0 and n % d == 0
    row_blk = pl.BlockSpec((tm, d), lambda i: (i, 0))
    in_specs = [row_blk, pl.BlockSpec((1, d), lambda i: (0, 0)), pl.BlockSpec((d, n), lambda i: (0, 0))]
    args = [x2, gain.reshape(1, d), w]
    if hgrn_layer is None:
        dtypes = [BF16] * (n // d)
    else:
        assert n == 4 * d
        in_specs.append(pl.BlockSpec(lower_bound.shape, lambda i: (0, 0)))
        args.append(lower_bound)
        dtypes = [BF16, F32, BF16, BF16, BF16]
    return pl.pallas_call(
        functools.partial(_proj_kernel, hgrn_layer=hgrn_layer),
        grid=(t // tm,),
        in_specs=in_specs,
        out_specs=[row_blk] * len(dtypes),
        out_shape=[jax.ShapeDtypeStruct((t, d), dt) for dt in dtypes],
        compiler_params=_params("parallel"),
        name="norm_proj_hgrn" if hgrn_layer is not None else "norm_proj",
    )(*args)


def _out_proj_kernel(a_ref, w_ref, x_ref, o_ref):
    o_ref[...] = x_ref[...] + _dot(a_ref[...], w_ref[...])


def _out_proj(a2, w, x2):
    t, d = x2.shape
    tm = TOKEN_TILE
    return pl.pallas_call(
        _out_proj_kernel,
        grid=(t // tm,),
        in_specs=[
            pl.BlockSpec((tm, a2.shape[1]), lambda i: (i, 0)),
            pl.BlockSpec(w.shape, lambda i: (0, 0)),
            pl.BlockSpec((tm, d), lambda i: (i, 0)),
        ],
        out_specs=pl.BlockSpec((tm, d), lambda i: (i, 0)),
        out_shape=jax.ShapeDtypeStruct((t, d), F32),
        compiler_params=_params("parallel"),
        name="out_proj",
    )(a2, w, x2)


def _mlp_kernel(x_ref, g_ref, w1_ref, w2_ref, *rest, final_norm):
    if final_norm:
        gf_ref, o_ref, h_scr, acc_scr = rest
    else:
        o_ref, h_scr, acc_scr = rest
    j = pl.program_id(1)

    @pl.when(j == 0)
    def _():
        h_scr[...] = _rms(x_ref[...], g_ref[...]).astype(BF16)
        acc_scr[...] = jnp.zeros_like(acc_scr)

    a = jnp.maximum(_dot(h_scr[...], w1_ref[...]), 0.0)
    acc_scr[...] += _dot((a * a).astype(BF16), w2_ref[...])

    @pl.when(j == pl.num_programs(1) - 1)
    def _():
        y = x_ref[...] + acc_scr[...]
        o_ref[...] = _rms(y, gf_ref[...]) if final_norm else y


def _mlp(x2, gain, w1, w2, final_gain=None):
    t, d = x2.shape
    f = w1.shape[1]
    tm, tf = TOKEN_TILE, FF_TILE
    final_norm = final_gain is not None
    in_specs = [
        pl.BlockSpec((tm, d), lambda i, j: (i, 0)),
        pl.BlockSpec((1, d), lambda i, j: (0, 0)),
        pl.BlockSpec((d, tf), lambda i, j: (0, j)),
        pl.BlockSpec((tf, d), lambda i, j: (j, 0)),
    ]
    args = [x2, gain.reshape(1, d), w1, w2]
    if final_norm:
        in_specs.append(pl.BlockSpec((1, d), lambda i, j: (0, 0)))
        args.append(final_gain.reshape(1, d))
    return pl.pallas_call(
        functools.partial(_mlp_kernel, final_norm=final_norm),
        grid=(t // tm, f // tf),
        in_specs=in_specs,
        out_specs=pl.BlockSpec((tm, d), lambda i, j: (i, 0)),
        out_shape=jax.ShapeDtypeStruct((t, d), F32),
        scratch_shapes=[pltpu.VMEM((tm, d), BF16), pltpu.VMEM((tm, d), F32)],
        compiler_params=_params("parallel", "arbitrary"),
        name="mlp",
    )(*args)


def _gla_levels():
    out, blk = [], A_CHUNK
    while blk >= 2:
        out.append(blk)
        blk //= 2
    return out


def _gla_constants():
    t = np.arange(A_CHUNK)
    sgn, msk = [], []
    for blk in _gla_levels():
        second = (t % blk) >= blk // 2
        sgn.append(np.broadcast_to(np.where(second, 1.0, -1.0)[:, None], (A_CHUNK, A_HEAD)))
        same = (t[:, None] // blk) == (t[None, :] // blk)
        msk.append(same & second[:, None] & (~second)[None, :])
    msk.append(np.eye(A_CHUNK, dtype=bool))
    return jnp.asarray(np.stack(sgn), F32), jnp.asarray(np.stack(msk), F32)


def _bdot_nt(a, b):
    return lax.dot_general(a, b, (((2,), (2,)), ((0,), (0,))), preferred_element_type=F32)


def _gla_kernel(qa_ref, lf_ref, k_ref, v_ref, sg_ref, hn_ref, sgn_ref, msk_ref, o_ref, st_scr, b_scr):
    c_len, grp = A_CHUNK, A_GROUP
    g_len = grp * c_len
    nv = c_len // 8
    seq, dk = qa_ref.shape[1], qa_ref.shape[2]
    levels = _gla_levels()
    sub = lax.broadcasted_iota(jnp.int32, (grp * nv, 8, dk), 1)
    sub8 = lax.broadcasted_iota(jnp.int32, (8, dk), 0)
    st_scr[...] = jnp.zeros_like(st_scr)

    def group(gi, carry):
        r0 = pl.multiple_of(gi * g_len, g_len)
        rows = pl.ds(r0, g_len)
        q = qa_ref[0, rows, :].astype(F32)
        k = k_ref[0, rows, :].astype(F32)
        v3 = v_ref[0, rows, :].reshape(grp, c_len, dk)

        x3 = lf_ref[0, rows, :].reshape(grp * nv, 8, dk)
        for sh in (1, 2, 4):
            x3 = x3 + jnp.where(sub >= sh, pltpu.roll(x3, sh, 1), 0.0)
        tot = x3[:, 7:8, :]
        pieces = []
        for g in range(grp):
            run = None
            for j in range(nv):
                i = g * nv + j
                pieces.append(x3[i] if run is None else x3[i] + run)
                run = tot[i] if run is None else run + tot[i]
        b2 = jnp.concatenate(pieces, axis=0)
        b_scr[rows, :] = b2

        def row_b(r, n):
            return jnp.broadcast_to(b_scr[pl.ds(r0 + r, 1), :], (n, dk))

        q3 = q.reshape(grp, c_len, dk)
        k3 = k.reshape(grp, c_len, dk)
        b3 = b2.reshape(grp, c_len, dk)

        attn = _bdot_nt(q3.astype(BF16), k3.astype(BF16)) * msk_ref[len(levels)][None]
        for li, blk in enumerate(levels):
            half = blk // 2
            parts = []
            for g in range(grp):
                for j in range(nv):
                    base = g * c_len + ((8 * j // blk) * blk if blk >= 8 else 8 * j)
                    val = row_b(base + half - 1, 8)
                    for s_i in range(1, 8 // blk):
                        val = jnp.where(sub8 >= s_i * blk, row_b(base + s_i * blk + half - 1, 8), val)
                    parts.append(val)
            b_mid = jnp.concatenate(parts, axis=0).reshape(grp, c_len, dk)
            sg = sgn_ref[li][None]
            x_l = (jnp.where(sg > 0.0, q3, k3) * jnp.exp2((b3 - b_mid) * sg)).astype(BF16)
            attn = attn + _bdot_nt(x_l, x_l) * msk_ref[li][None]

        o_in = lax.dot_general(attn.astype(BF16), v3, (((2,), (1,)), ((0,), (0,))),
                               preferred_element_type=F32)
        b_last = jnp.concatenate([row_b(g * c_len + c_len - 1, c_len) for g in range(grp)], axis=0)
        k_dec = (k * jnp.exp2(b_last - b2)).astype(BF16).reshape(grp, c_len, dk)
        upd = lax.dot_general(v3, k_dec, (((1,), (1,)), ((0,), (0,))), preferred_element_type=F32)
        qe = (q * jnp.exp2(b2)).astype(BF16).reshape(grp, c_len, dk)

        st = st_scr[...]
        sts = []
        for g in range(grp):
            sts.append(st.astype(BF16))
            st = st * jnp.exp2(row_b(g * c_len + c_len - 1, 1)) + upd[g]
        st_scr[...] = st
        o = (o_in + _bdot_nt(qe, jnp.stack(sts, axis=0))).reshape(g_len, dk)
        ms = jnp.mean(o * o, axis=-1, keepdims=True)
        o = o * lax.rsqrt(ms + RMS_EPS) * hn_ref[...]
        o_ref[0, rows, :] = (o * sg_ref[0, rows, :].astype(F32)).astype(o_ref.dtype)
        return carry

    lax.fori_loop(0, seq // g_len, group, 0)


def _gla(qa, lf2, k, v, sg, head_norm):
    bsz, seq, d = qa.shape
    assert seq % (A_GROUP * A_CHUNK) == 0 and d % A_HEAD == 0
    head = pl.BlockSpec((1, seq, A_HEAD), lambda b, h: (b, 0, h))
    sgn, msk = _gla_constants()
    return pl.pallas_call(
        _gla_kernel,
        grid=(bsz, d // A_HEAD),
        in_specs=[head] * 5 + [pl.BlockSpec((1, A_HEAD), lambda b, h: (0, 0)),
                               pl.BlockSpec(sgn.shape, lambda b, h: (0, 0, 0)),
                               pl.BlockSpec(msk.shape, lambda b, h: (0, 0, 0))],
        out_specs=head,
        out_shape=jax.ShapeDtypeStruct((bsz, seq, d), BF16),
        scratch_shapes=[pltpu.VMEM((A_HEAD, A_HEAD), F32), pltpu.VMEM((seq, A_HEAD), F32)],
        compiler_params=_params("parallel", "parallel"),
        name="gla",
    )(qa, lf2, k, v, sg, head_norm.reshape(1, A_HEAD), sgn, msk)


def _t5_large_thresholds():
    max_exact = REL_BUCKETS // 2
    n = np.arange(max_exact, 4 * REL_MAX_DISTANCE, dtype=np.float64)
    large = max_exact + (np.log(n / max_exact) / math.log(REL_MAX_DISTANCE / max_exact)
                         * (REL_BUCKETS - max_exact)).astype(np.int64)
    large = np.minimum(large, REL_BUCKETS - 1)
    assert np.all(np.diff(large) >= 0)
    return [int(n[np.argmax(large >= b)]) for b in range(max_exact + 1, REL_BUCKETS)]


def _moba_kernel(tab_ref, q_ref, k_ref, v_ref, o_ref):
    h = pl.program_id(1)
    seq, dh = q_ref.shape[1], q_ref.shape[2]
    nb = seq // B_BLOCK
    scale = 1.0 / math.sqrt(dh)
    q16 = q_ref[0]
    k16 = k_ref[0]
    v16 = v_ref[0]
    q = q16.astype(F32)
    k = k16.astype(F32)

    k_mean = jnp.concatenate(
        [jnp.mean(k[n * B_BLOCK:(n + 1) * B_BLOCK], axis=0, keepdims=True) for n in range(nb)], axis=0)
    gate = lax.dot_general(k_mean, q, (((1,), (1,)), ((), ())),
                           precision=lax.Precision.HIGHEST, preferred_element_type=F32)
    blk_id = lax.broadcasted_iota(jnp.int32, (nb, seq), 0)
    tok_blk = lax.broadcasted_iota(jnp.int32, (nb, seq), 1) // B_BLOCK
    past = blk_id < tok_blk
    gate = jnp.where(past, gate, -jnp.inf)
    rank = jnp.zeros((nb, seq), jnp.int32)
    for m in range(nb):
        g_m = gate[m:m + 1, :]
        ahead = jnp.where(g_m > gate, 1, jnp.where(jnp.logical_and(g_m == gate, blk_id > m), 1, 0))
        rank = rank + ahead
    sel = jnp.where(past, jnp.where(rank < B_TOPK, 1.0, 0.0), 0.0)
    sel_t = jnp.concatenate([sel, jnp.zeros((128 - nb, seq), F32)], axis=0).T

    width = seq + B_BLOCK
    dist = seq - lax.broadcasted_iota(jnp.int32, (1, width), 1)
    n_pos = jnp.maximum(dist, 0)
    bucket = jnp.minimum(n_pos, REL_BUCKETS // 2)
    for thr in _t5_large_thresholds():
        bucket = bucket + jnp.where(n_pos >= thr, 1, 0)
    rvec = jnp.full((1, width), -jnp.inf, F32)
    for bkt in range(REL_BUCKETS):
        rvec = jnp.where(jnp.logical_and(bucket == bkt, dist >= 0), tab_ref[h, bkt], rvec)

    for qt in range(nb):
        w_keys = B_BLOCK * (qt + 1)
        a0 = seq - B_BLOCK * qt
        rows = slice(qt * B_BLOCK, (qt + 1) * B_BLOCK)
        s = _dot_nt(q16[rows], k16[:w_keys]) * scale
        wrow = jnp.concatenate([rvec[:, a0:a0 + w_keys], rvec[:, a0 - B_BLOCK:a0]], axis=1)
        tile = pltpu.roll(jnp.broadcast_to(wrow, (B_BLOCK, w_keys + B_BLOCK)), 0, 1,
                          stride=1, stride_axis=0)
        s = s + tile[:, :w_keys]
        sel_q = sel_t[rows]
        pieces = [jnp.where(sel_q[:, n:n + 1] > 0.5, s[:, n * B_BLOCK:(n + 1) * B_BLOCK], -jnp.inf)
                  for n in range(qt)]
        pieces.append(s[:, qt * B_BLOCK:])
        s = jnp.concatenate(pieces, axis=1) if qt else pieces[0]
        m = jnp.max(s, axis=1, keepdims=True)
        p = jnp.exp(s - m)
        l = jnp.sum(p, axis=1, keepdims=True)
        o = _dot(p.astype(BF16), v16[:w_keys]) / l
        o_ref[0, rows, :] = o.astype(o_ref.dtype)


def _moba(q, k, v, rel_bias):
    bsz, seq, d = q.shape
    assert seq % B_BLOCK == 0 and seq // B_BLOCK <= 128 and d % B_HEAD == 0
    head = pl.BlockSpec((1, seq, B_HEAD), lambda b, h: (b, 0, h))
    return pl.pallas_call(
        _moba_kernel,
        grid=(bsz, d // B_HEAD),
        in_specs=[pl.BlockSpec(memory_space=pltpu.SMEM), head, head, head],
        out_specs=head,
        out_shape=jax.ShapeDtypeStruct((bsz, seq, d), BF16),
        compiler_params=_params("parallel", "parallel"),
        name="moba",
    )(rel_bias.T.astype(F32), q, k, v)


def _pool_kernel(x_ref, xp_ref, g_ref, w_ref, sc_ref, o_ref):
    i = pl.program_id(1)
    ts = x_ref.shape[1]
    x = x_ref[0]
    gain = g_ref[...]
    h = _rms(x, gain)
    h_prev = jnp.where(i > 0, _rms(xp_ref[0], gain), 0.0)
    hc = jnp.concatenate([h_prev, h], axis=0)
    pos1 = (i * ts + 1 + lax.broadcasted_iota(jnp.int32, (ts, 1), 0)).astype(F32)
    cg = x.shape[1] // len(C_WINDOWS)
    outs = []
    for g, w in enumerate(C_WINDOWS):
        cols = slice(g * cg, (g + 1) * cg)
        acc = hc[:, cols]
        span = 1
        while span < w:
            acc = acc + pltpu.roll(acc, span, 0)
            span *= 2
        pooled = acc[C_HALO:] / jnp.minimum(pos1, float(w))
        outs.append(_dot((pooled - h[:, cols]).astype(BF16), w_ref[g]))
    y = jnp.concatenate(outs, axis=1) * sc_ref[...]
    o_ref[0] = x + y


def _pool(x3, gain, w_group, scale):
    bsz, seq, d = x3.shape
    ts = POOL_TILE
    assert seq % ts == 0 and ts % C_HALO == 0 and max(C_WINDOWS) <= C_HALO
    per = ts // C_HALO
    return pl.pallas_call(
        _pool_kernel,
        grid=(bsz, seq // ts),
        in_specs=[
            pl.BlockSpec((1, ts, d), lambda b, i: (b, i, 0)),
            pl.BlockSpec((1, C_HALO, d), lambda b, i: (b, jnp.maximum(i * per - 1, 0), 0)),
            pl.BlockSpec((1, d), lambda b, i: (0, 0)),
            pl.BlockSpec(w_group.shape, lambda b, i: (0, 0, 0)),
            pl.BlockSpec((1, d), lambda b, i: (0, 0)),
        ],
        out_specs=pl.BlockSpec((1, ts, d), lambda b, i: (b, i, 0)),
        out_shape=jax.ShapeDtypeStruct((bsz, seq, d), F32),
        compiler_params=_params("parallel", "parallel"),
        name="pool",
    )(x3, x3, gain.reshape(1, d), w_group, scale.reshape(1, d))


def kernel(x, norm_mix, norm_mlp, norm_final, a_w_in, a_lower_bound, a_head_norm, a_w_out,
           b_w_qkv, b_w_out, rel_bias, c_w_group, c_scale, mlp_w1, mlp_w2):
    bsz, seq, d = x.shape
    depth = norm_mix.shape[0]
    t = bsz * seq
    x2 = x.reshape(t, d)
    for i in range(depth):
        mixer, j = i % N_MIXERS, i // N_MIXERS
        if mixer == 0:
            parts = _norm_proj(x2, norm_mix[i], a_w_in[j].astype(BF16),
                               lower_bound=a_lower_bound.astype(F32), hgrn_layer=j)
            og = _gla(*[p.reshape(bsz, seq, d) for p in parts], a_head_norm[j])
            x2 = _out_proj(og.reshape(t, d), a_w_out[j].astype(BF16), x2)
        elif mixer == 1:
            qkv = _norm_proj(x2, norm_mix[i], b_w_qkv[j].astype(BF16))
            o = _moba(*[p.reshape(bsz, seq, d) for p in qkv], rel_bias)
            x2 = _out_proj(o.reshape(t, d), b_w_out[j].astype(BF16), x2)
        else:
            x2 = _pool(x2.reshape(bsz, seq, d), norm_mix[i], c_w_group[j].astype(BF16),
                       c_scale[j]).reshape(t, d)
        x2 = _mlp(x2, norm_mlp[i], mlp_w1[i].astype(BF16), mlp_w2[i].astype(BF16),
                  final_gain=norm_final if i == depth - 1 else None)
    return x2.reshape(bsz, seq, d)
```

```python
import functools
import math

import numpy as np
import jax
import jax.numpy as jnp
from jax import lax
from jax.experimental import pallas as pl
from jax.experimental.pallas import tpu as pltpu

F32 = jnp.float32
BF16 = jnp.bfloat16

RMS_EPS = 1e-6
N_MIXERS = 3

A_HEAD = 128
A_CHUNK = 64
A_GROUP = 8
LOG2E = 1.4426950408889634

B_HEAD = 128
B_BLOCK = 256
B_TOPK = 3
REL_BUCKETS = 32
REL_MAX_DISTANCE = 1024
B_MASK_SHIFT = -2.0 ** 100
B_AHEAD = 2

C_WINDOWS = (2, 4, 8, 16)
C_HALO = 16

VMEM_LIMIT_BYTES = 56 * 1024 * 1024

TOKEN_TILE = 1024
PROJ_TILE = 512
FF_TILE = 1024
POOL_TILE = 512


def _params(*semantics):
    return pltpu.CompilerParams(dimension_semantics=semantics,
                                vmem_limit_bytes=VMEM_LIMIT_BYTES)


def _rms(x, gain):
    ms = jnp.mean(x * x, axis=-1, keepdims=True)
    return x * lax.rsqrt(ms + RMS_EPS) * gain


def _dot(a, b):
    return jnp.dot(a, b, preferred_element_type=F32)


def _dot_nt(a, b):
    return lax.dot_general(a, b, (((1,), (1,)), ((), ())), preferred_element_type=F32)


def _dot_tn(a, b):
    return lax.dot_general(a, b, (((0,), (0,)), ((), ())), preferred_element_type=F32)


def _hgrn_lower_bound(lb_ref, layer):
    a = lb_ref[...]
    e = jnp.exp(a - jnp.max(a, axis=0, keepdims=True))
    sm = e / jnp.sum(e, axis=0, keepdims=True)
    cum = jnp.sum(sm[:layer + 1], axis=0, keepdims=True)
    return cum - sm[0:1]


def _silu(y):
    return (0.5 * y) * (1.0 + jnp.tanh(0.5 * y))


def _proj_kernel(x_ref, g_ref, w_ref, *rest, hgrn_layer):
    d = x_ref.shape[1]
    h = _rms(x_ref[...], g_ref[...]).astype(BF16)

    def section(s):
        return _dot(h, w_ref[:, s * d:(s + 1) * d])

    if hgrn_layer is None:
        for s, o_ref in enumerate(rest):
            o_ref[...] = section(s).astype(o_ref.dtype)
        return

    lb_ref, qa_ref, lf_ref, k_ref, v_ref, sg_ref = rest
    y_q, y, y_v, y_g = [section(s) for s in range(4)]
    qa_ref[...] = _silu(y_q).astype(qa_ref.dtype)
    v_ref[...] = y_v.astype(v_ref.dtype)
    sg_ref[...] = _silu(y_g).astype(sg_ref.dtype)

    lb = _hgrn_lower_bound(lb_ref, hgrn_layer)
    a = jnp.log2(lb)
    u = jnp.exp2(jnp.abs(y) * (-LOG2E))
    w1 = 1.0 + u
    c = jnp.log2(1.0 - lb) + (jnp.minimum(y, 0.0) * LOG2E - jnp.log2(w1))
    lf_ref[...] = jnp.maximum(a, c) + jnp.log2(1.0 + jnp.exp2(-jnp.abs(a - c)))
    k_ref[...] = ((1.0 - lb) * jnp.where(y >= 0.0, u, 1.0) / w1).astype(k_ref.dtype)


def _norm_proj(x2, gain, w, lower_bound=None, hgrn_layer=None):
    t, d = x2.shape
    n = w.shape[1]
    tm = PROJ_TILE
    assert t % tm == 0 and n % d == 0
    row_blk = pl.BlockSpec((tm, d), lambda i: (i, 0))
    in_specs = [row_blk, pl.BlockSpec((1, d), lambda i: (0, 0)), pl.BlockSpec((d, n), lambda i: (0, 0))]
    args = [x2, gain.reshape(1, d), w]
    if hgrn_layer is None:
        dtypes = [BF16] * (n // d)
    else:
        assert n == 4 * d
        in_specs.append(pl.BlockSpec(lower_bound.shape, lambda i: (0, 0)))
        args.append(lower_bound)
        dtypes = [BF16, F32, BF16, BF16, BF16]
    return pl.pallas_call(
        functools.partial(_proj_kernel, hgrn_layer=hgrn_layer),
        grid=(t // tm,),
        in_specs=in_specs,
        out_specs=[row_blk] * len(dtypes),
        out_shape=[jax.ShapeDtypeStruct((t, d), dt) for dt in dtypes],
        compiler_params=_params("parallel"),
        name="norm_proj_hgrn" if hgrn_layer is not None else "norm_proj",
    )(*args)


def _mlp_kernel(x_ref, g_ref, w1_ref, w2_ref, *rest, has_proj, final_norm):
    rest = list(rest)
    a_ref, wo_ref = (rest.pop(0), rest.pop(0)) if has_proj else (None, None)
    gf_ref = rest.pop(0) if final_norm else None
    o_ref, h_scr, acc_scr = rest
    j = pl.program_id(1)

    @pl.when(j == 0)
    def _():
        x = x_ref[...]
        if has_proj:
            x = x + _dot(a_ref[...], wo_ref[...])
        h_scr[...] = _rms(x, g_ref[...]).astype(BF16)
        acc_scr[...] = x

    a = jnp.maximum(_dot(h_scr[...], w1_ref[...]), 0.0)
    acc_scr[...] += _dot((a * a).astype(BF16), w2_ref[...])

    @pl.when(j == pl.num_programs(1) - 1)
    def _():
        y = acc_scr[...]
        o_ref[...] = _rms(y, gf_ref[...]) if final_norm else y


def _mlp(x2, gain, w1, w2, mixer_out=None, w_out=None, final_gain=None):
    t, d = x2.shape
    f = w1.shape[1]
    tm, tf = TOKEN_TILE, FF_TILE
    has_proj = mixer_out is not None
    final_norm = final_gain is not None
    row_blk = pl.BlockSpec((tm, d), lambda i, j: (i, 0))
    vec = pl.BlockSpec((1, d), lambda i, j: (0, 0))
    in_specs = [row_blk, vec,
                pl.BlockSpec((d, tf), lambda i, j: (0, j)),
                pl.BlockSpec((tf, d), lambda i, j: (j, 0))]
    args = [x2, gain.reshape(1, d), w1, w2]
    if has_proj:
        in_specs += [row_blk, pl.BlockSpec(w_out.shape, lambda i, j: (0, 0))]
        args += [mixer_out, w_out]
    if final_norm:
        in_specs.append(vec)
        args.append(final_gain.reshape(1, d))
    return pl.pallas_call(
        functools.partial(_mlp_kernel, has_proj=has_proj, final_norm=final_norm),
        grid=(t // tm, f // tf),
        in_specs=in_specs,
        out_specs=row_blk,
        out_shape=jax.ShapeDtypeStruct((t, d), F32),
        scratch_shapes=[pltpu.VMEM((tm, d), BF16), pltpu.VMEM((tm, d), F32)],
        compiler_params=_params("parallel", "arbitrary"),
        name="mlp",
    )(*args)


def _gla_levels():
    out, blk = [], A_CHUNK
    while blk >= 2:
        out.append(blk)
        blk //= 2
    return out


def _gla_constants():
    t = np.arange(A_CHUNK)
    sgn, msk = [], []
    for blk in _gla_levels():
        second = (t % blk) >= blk // 2
        sgn.append(np.broadcast_to(np.where(second, 1.0, -1.0)[:, None], (A_CHUNK, A_HEAD)))
        same = (t[:, None] // blk) == (t[None, :] // blk)
        msk.append(same & second[:, None] & (~second)[None, :])
    msk.append(np.eye(A_CHUNK, dtype=bool))
    return jnp.asarray(np.stack(sgn), F32), jnp.asarray(np.stack(msk), F32)


def _bdot_nt(a, b):
    return lax.dot_general(a, b, (((2,), (2,)), ((0,), (0,))), preferred_element_type=F32)


def _gla_kernel(qa_ref, lf_ref, k_ref, v_ref, sg_ref, hn_ref, sgn_ref, msk_ref, o_ref, st_scr, b_scr):
    c_len, grp = A_CHUNK, A_GROUP
    g_len = grp * c_len
    nv = c_len // 8
    seq, dk = qa_ref.shape[1], qa_ref.shape[2]
    levels = _gla_levels()
    sub = lax.broadcasted_iota(jnp.int32, (grp * nv, 8, dk), 1)
    sub8 = lax.broadcasted_iota(jnp.int32, (8, dk), 0)
    st_scr[...] = jnp.zeros_like(st_scr)

    def group(gi, carry):
        r0 = pl.multiple_of(gi * g_len, g_len)
        rows = pl.ds(r0, g_len)
        q = qa_ref[0, rows, :].astype(F32)
        k = k_ref[0, rows, :].astype(F32)
        v3 = v_ref[0, rows, :].reshape(grp, c_len, dk)

        x3 = lf_ref[0, rows, :].reshape(grp * nv, 8, dk)
        for sh in (1, 2, 4):
            x3 = x3 + jnp.where(sub >= sh, pltpu.roll(x3, sh, 1), 0.0)
        tot = x3[:, 7:8, :]
        pieces = []
        for g in range(grp):
            run = None
            for j in range(nv):
                i = g * nv + j
                pieces.append(x3[i] if run is None else x3[i] + run)
                run = tot[i] if run is None else run + tot[i]
        b2 = jnp.concatenate(pieces, axis=0)
        b_scr[rows, :] = b2

        def row_b(r, n):
            return jnp.broadcast_to(b_scr[pl.ds(r0 + r, 1), :], (n, dk))

        q3 = q.reshape(grp, c_len, dk)
        k3 = k.reshape(grp, c_len, dk)
        b3 = b2.reshape(grp, c_len, dk)

        attn = _bdot_nt(q3.astype(BF16), k3.astype(BF16)) * msk_ref[len(levels)][None]
        for li, blk in enumerate(levels):
            half = blk // 2
            parts = []
            for g in range(grp):
                for j in range(nv):
                    base = g * c_len + ((8 * j // blk) * blk if blk >= 8 else 8 * j)
                    val = row_b(base + half - 1, 8)
                    for s_i in range(1, 8 // blk):
                        val = jnp.where(sub8 >= s_i * blk, row_b(base + s_i * blk + half - 1, 8), val)
                    parts.append(val)
            b_mid = jnp.concatenate(parts, axis=0).reshape(grp, c_len, dk)
            sg = sgn_ref[li][None]
            x_l = (jnp.where(sg > 0.0, q3, k3) * jnp.exp2((b3 - b_mid) * sg)).astype(BF16)
            attn = attn + _bdot_nt(x_l, x_l) * msk_ref[li][None]

        o_in = lax.dot_general(attn.astype(BF16), v3, (((2,), (1,)), ((0,), (0,))),
                               preferred_element_type=F32)
        b_last = jnp.concatenate([row_b(g * c_len + c_len - 1, c_len) for g in range(grp)], axis=0)
        k_dec = (k * jnp.exp2(b_last - b2)).astype(BF16).reshape(grp, c_len, dk)
        upd = lax.dot_general(v3, k_dec, (((1,), (1,)), ((0,), (0,))), preferred_element_type=F32)
        qe = (q * jnp.exp2(b2)).astype(BF16).reshape(grp, c_len, dk)

        st = st_scr[...]
        sts = []
        for g in range(grp):
            sts.append(st.astype(BF16))
            st = st * jnp.exp2(row_b(g * c_len + c_len - 1, 1)) + upd[g]
        st_scr[...] = st
        o = (o_in + _bdot_nt(qe, jnp.stack(sts, axis=0))).reshape(g_len, dk)
        ms = jnp.mean(o * o, axis=-1, keepdims=True)
        o = o * lax.rsqrt(ms + RMS_EPS) * hn_ref[...]
        o_ref[0, rows, :] = (o * sg_ref[0, rows, :].astype(F32)).astype(o_ref.dtype)
        return carry

    lax.fori_loop(0, seq // g_len, group, 0)


def _gla(qa, lf2, k, v, sg, head_norm):
    bsz, seq, d = qa.shape
    assert seq % (A_GROUP * A_CHUNK) == 0 and d % A_HEAD == 0
    head = pl.BlockSpec((1, seq, A_HEAD), lambda b, h: (b, 0, h))
    sgn, msk = _gla_constants()
    return pl.pallas_call(
        _gla_kernel,
        grid=(bsz, d // A_HEAD),
        in_specs=[head] * 5 + [pl.BlockSpec((1, A_HEAD), lambda b, h: (0, 0)),
                               pl.BlockSpec(sgn.shape, lambda b, h: (0, 0, 0)),
                               pl.BlockSpec(msk.shape, lambda b, h: (0, 0, 0))],
        out_specs=head,
        out_shape=jax.ShapeDtypeStruct((bsz, seq, d), BF16),
        scratch_shapes=[pltpu.VMEM((A_HEAD, A_HEAD), F32), pltpu.VMEM((seq, A_HEAD), F32)],
        compiler_params=_params("parallel", "parallel"),
        name="gla",
    )(qa, lf2, k, v, sg, head_norm.reshape(1, A_HEAD), sgn, msk)


def _t5_large_thresholds():
    max_exact = REL_BUCKETS // 2
    n = np.arange(max_exact, 4 * REL_MAX_DISTANCE, dtype=np.float64)
    large = max_exact + (np.log(n / max_exact) / math.log(REL_MAX_DISTANCE / max_exact)
                         * (REL_BUCKETS - max_exact)).astype(np.int64)
    large = np.minimum(large, REL_BUCKETS - 1)
    assert np.all(np.diff(large) >= 0)
    return [int(n[np.argmax(large >= b)]) for b in range(max_exact + 1, REL_BUCKETS)]


def _moba_kernel(tab_ref, q_ref, k_ref, v_ref, o_ref):
    h = pl.program_id(1)
    seq, dh = q_ref.shape[1], q_ref.shape[2]
    nb = seq // B_BLOCK
    q16 = q_ref[0]
    k16 = k_ref[0]
    v16 = v_ref[0]
    blk_rows = lax.broadcasted_iota(jnp.int32, (dh, seq), 0)
    key_blk = lax.broadcasted_iota(jnp.int32, (dh, seq), 1) // B_BLOCK
    blk_onehot = jnp.where(blk_rows == key_blk, 1.0, 0.0).astype(BF16)

    k_mean = _dot(blk_onehot[:16], k16)[:nb] * (1.0 / B_BLOCK)
    km_hi = k_mean.astype(BF16).astype(F32)
    km_mid = (k_mean - km_hi).astype(BF16).astype(F32)
    km_lo = (k_mean - km_hi - km_mid).astype(BF16).astype(F32)
    km_terms = jnp.concatenate([km_hi, km_mid, km_lo, jnp.zeros_like(km_hi)], axis=0).astype(BF16)
    gate3 = _dot_nt(km_terms, q16)
    gate = gate3[:nb] + gate3[nb:2 * nb] + gate3[2 * nb:3 * nb]
    blk_id = lax.broadcasted_iota(jnp.int32, (nb, seq), 0)
    tok_blk = lax.broadcasted_iota(jnp.int32, (nb, seq), 1) // B_BLOCK
    past = blk_id < tok_blk
    gate = jnp.where(past, gate, -jnp.inf)
    rank = jnp.zeros((nb, seq), jnp.int32)
    for m in range(nb):
        g_m = gate[m:m + 1, :]
        ahead = jnp.where(g_m > gate, 1, jnp.where(jnp.logical_and(g_m == gate, blk_id > m), 1, 0))
        rank = rank + ahead
    shift = jnp.where(past, jnp.where(rank < B_TOPK, 0.0, B_MASK_SHIFT),
                      jnp.where(blk_id == tok_blk, 0.0, B_MASK_SHIFT))
    shift_t = jnp.concatenate([shift, jnp.zeros((dh - nb, seq), F32)], axis=0).T.astype(BF16)
    q_aug = jnp.concatenate([q16, shift_t], axis=1)
    kt_aug = jnp.concatenate([k16.T, blk_onehot], axis=0)
    lane = lax.broadcasted_iota(jnp.int32, (seq, dh), 1)
    v_aug = jnp.concatenate([v16, jnp.where(lane == 0, 1.0, 0.0).astype(BF16)], axis=1)

    width = seq + B_BLOCK
    dist = seq - lax.broadcasted_iota(jnp.int32, (1, width), 1)
    n_pos = jnp.maximum(dist, 0)
    bucket = jnp.minimum(n_pos, REL_BUCKETS // 2)
    for thr in _t5_large_thresholds():
        bucket = bucket + jnp.where(n_pos >= thr, 1, 0)
    rvec = jnp.full((1, width), -jnp.inf, F32)
    for bkt in range(REL_BUCKETS):
        rvec = jnp.where(jnp.logical_and(bucket == bkt, dist >= 0), tab_ref[h, bkt] * LOG2E, rvec)
    tiles = []
    for dl in range(nb):
        a0 = seq - B_BLOCK * dl
        wrow = jnp.concatenate([rvec[:, a0:a0 + B_BLOCK], rvec[:, a0 - B_BLOCK:a0]], axis=1)
        tiles.append(pltpu.roll(jnp.broadcast_to(wrow, (B_BLOCK, 2 * B_BLOCK)), 0, 1,
                                stride=1, stride_axis=0)[:, :B_BLOCK])

    score_scale = LOG2E / math.sqrt(dh)

    def score_block(qt, n):
        q_blk = q_aug[qt * B_BLOCK:(qt + 1) * B_BLOCK]
        return _dot(q_blk, kt_aug[:, n * B_BLOCK:(n + 1) * B_BLOCK]) * score_scale + tiles[qt - n]

    def row_max(blocks):
        mx = None
        for s2 in blocks:
            blk_max = jnp.maximum(s2[:, :dh], s2[:, dh:])
            mx = blk_max if mx is None else jnp.maximum(mx, blk_max)
        return jnp.max(mx, axis=1, keepdims=True)

    def attend(qt, blocks):
        m = row_max(blocks)
        acc = None
        for n, s2 in enumerate(blocks):
            part = _dot(jnp.exp2(s2 - m).astype(BF16), v_aug[n * B_BLOCK:(n + 1) * B_BLOCK])
            acc = part if acc is None else acc + part
        rows = slice(qt * B_BLOCK, (qt + 1) * B_BLOCK)
        o_ref[0, rows, :] = (acc[:, :dh] / acc[:, dh:dh + 1]).astype(o_ref.dtype)

    ready = [[score_block(t, n) for n in range(t + 1)] for t in range(min(B_AHEAD, nb))]
    for qt in range(nb):
        if qt + B_AHEAD < nb:
            ready.append([score_block(qt + B_AHEAD, n) for n in range(qt + B_AHEAD + 1)])
        attend(qt, ready.pop(0))


def _moba(q, k, v, rel_bias):
    bsz, seq, d = q.shape
    assert seq % B_BLOCK == 0 and seq // B_BLOCK <= 128 and d % B_HEAD == 0
    head = pl.BlockSpec((1, seq, B_HEAD), lambda b, h: (b, 0, h))
    return pl.pallas_call(
        _moba_kernel,
        grid=(bsz, d // B_HEAD),
        in_specs=[pl.BlockSpec(memory_space=pltpu.SMEM), head, head, head],
        out_specs=head,
        out_shape=jax.ShapeDtypeStruct((bsz, seq, d), BF16),
        compiler_params=_params("parallel", "parallel"),
        name="moba",
    )(rel_bias.T.astype(F32), q, k, v)


def _pool_kernel(x_ref, xp_ref, g_ref, w_ref, sc_ref, o_ref):
    i = pl.program_id(1)
    ts = x_ref.shape[1]
    x = x_ref[0]
    gain = g_ref[...]
    h = _rms(x, gain)
    h_prev = jnp.where(i > 0, _rms(xp_ref[0], gain), 0.0)
    hc = jnp.concatenate([h_prev, h], axis=0)
    pos1 = (i * ts + 1 + lax.broadcasted_iota(jnp.int32, (ts, 1), 0)).astype(F32)
    cg = x.shape[1] // len(C_WINDOWS)
    outs = []
    for g, w in enumerate(C_WINDOWS):
        cols = slice(g * cg, (g + 1) * cg)
        acc = hc[:, cols]
        span = 1
        while span < w:
            acc = acc + pltpu.roll(acc, span, 0)
            span *= 2
        pooled = acc[C_HALO:] / jnp.minimum(pos1, float(w))
        outs.append(_dot((pooled - h[:, cols]).astype(BF16), w_ref[g]))
    y = jnp.concatenate(outs, axis=1) * sc_ref[...]
    o_ref[0] = x + y


def _pool(x3, gain, w_group, scale):
    bsz, seq, d = x3.shape
    ts = POOL_TILE
    assert seq % ts == 0 and ts % C_HALO == 0 and max(C_WINDOWS) <= C_HALO
    per = ts // C_HALO
    return pl.pallas_call(
        _pool_kernel,
        grid=(bsz, seq // ts),
        in_specs=[
            pl.BlockSpec((1, ts, d), lambda b, i: (b, i, 0)),
            pl.BlockSpec((1, C_HALO, d), lambda b, i: (b, jnp.maximum(i * per - 1, 0), 0)),
            pl.BlockSpec((1, d), lambda b, i: (0, 0)),
            pl.BlockSpec(w_group.shape, lambda b, i: (0, 0, 0)),
            pl.BlockSpec((1, d), lambda b, i: (0, 0)),
        ],
        out_specs=pl.BlockSpec((1, ts, d), lambda b, i: (b, i, 0)),
        out_shape=jax.ShapeDtypeStruct((bsz, seq, d), F32),
        compiler_params=_params("parallel", "parallel"),
        name="pool",
    )(x3, x3, gain.reshape(1, d), w_group, scale.reshape(1, d))


def kernel(x, norm_mix, norm_mlp, norm_final, a_w_in, a_lower_bound, a_head_norm, a_w_out,
           b_w_qkv, b_w_out, rel_bias, c_w_group, c_scale, mlp_w1, mlp_w2):
    bsz, seq, d = x.shape
    depth = norm_mix.shape[0]
    t = bsz * seq
    x2 = x.reshape(t, d)
    for i in range(depth):
        mixer, j = i % N_MIXERS, i // N_MIXERS
        mixer_out = w_out = None
        if mixer == 0:
            parts = _norm_proj(x2, norm_mix[i], a_w_in[j].astype(BF16),
                               lower_bound=a_lower_bound.astype(F32), hgrn_layer=j)
            mixer_out = _gla(*[p.reshape(bsz, seq, d) for p in parts], a_head_norm[j]).reshape(t, d)
            w_out = a_w_out[j].astype(BF16)
        elif mixer == 1:
            qkv = _norm_proj(x2, norm_mix[i], b_w_qkv[j].astype(BF16))
            mixer_out = _moba(*[p.reshape(bsz, seq, d) for p in qkv], rel_bias).reshape(t, d)
            w_out = b_w_out[j].astype(BF16)
        else:
            x2 = _pool(x2.reshape(bsz, seq, d), norm_mix[i], c_w_group[j].astype(BF16),
                       c_scale[j]).reshape(t, d)
        x2 = _mlp(x2, norm_mlp[i], mlp_w1[i].astype(BF16), mlp_w2[i].astype(BF16),
                  mixer_out=mixer_out, w_out=w_out,
                  final_gain=norm_final if i == depth - 1 else None)
    return x2.reshape(bsz, seq, d)
```

```python
import functools
import math

import numpy as np
import jax
import jax.numpy as jnp
from jax import lax
from jax.experimental import pallas as pl
from jax.experimental.pallas import tpu as pltpu

F32 = jnp.float32
BF16 = jnp.bfloat16

RMS_EPS = 1e-6
N_MIXERS = 3

A_HEAD = 128
A_CHUNK = 64
A_GROUP = 8
LOG2E = 1.4426950408889634

B_HEAD = 128
B_BLOCK = 256
B_TOPK = 3
REL_BUCKETS = 32
REL_MAX_DISTANCE = 1024
B_MASK_SHIFT = -2.0 ** 100
B_AHEAD = 2

C_WINDOWS = (2, 4, 8, 16)
C_HALO = 16

VMEM_LIMIT_BYTES = 56 * 1024 * 1024

TOKEN_TILE = 512
FF_TILE = 1024
POOL_TILE = 512


def _params(*semantics):
    return pltpu.CompilerParams(dimension_semantics=semantics,
                                vmem_limit_bytes=VMEM_LIMIT_BYTES)


def _resident(shape):
    return pl.BlockSpec(shape, lambda *_: (0,) * len(shape), pipeline_mode=pl.Buffered(1))


def _rms(x, gain):
    ms = jnp.mean(x * x, axis=-1, keepdims=True)
    return x * lax.rsqrt(ms + RMS_EPS) * gain


def _dot(a, b):
    return jnp.dot(a, b, preferred_element_type=F32)


def _dot_nt(a, b):
    return lax.dot_general(a, b, (((1,), (1,)), ((), ())), preferred_element_type=F32)


def _dot_tn(a, b):
    return lax.dot_general(a, b, (((0,), (0,)), ((), ())), preferred_element_type=F32)


def _hgrn_lower_bound(lb_ref, layer):
    a = lb_ref[...]
    e = jnp.exp(a - jnp.max(a, axis=0, keepdims=True))
    sm = e / jnp.sum(e, axis=0, keepdims=True)
    cum = jnp.sum(sm[:layer + 1], axis=0, keepdims=True)
    return cum - sm[0:1]


def _silu(y):
    return (0.5 * y) * (1.0 + jnp.tanh(0.5 * y))


def _proj_kernel(x_ref, g_ref, w_ref, *rest, hgrn_layer):
    d = x_ref.shape[1]
    h = _rms(x_ref[...], g_ref[...]).astype(BF16)

    def section(s):
        return _dot(h, w_ref[:, s * d:(s + 1) * d])

    if hgrn_layer is None:
        for s, o_ref in enumerate(rest):
            o_ref[...] = section(s).astype(o_ref.dtype)
        return

    lb_ref, qa_ref, lf_ref, k_ref, v_ref, sg_ref = rest
    y_q, y, y_v, y_g = [section(s) for s in range(4)]
    qa_ref[...] = _silu(y_q).astype(qa_ref.dtype)
    v_ref[...] = y_v.astype(v_ref.dtype)
    sg_ref[...] = _silu(y_g).astype(sg_ref.dtype)

    lb = _hgrn_lower_bound(lb_ref, hgrn_layer)
    a = jnp.log2(lb)
    s = y * LOG2E
    c = jnp.log2(1.0 - lb) + (jnp.minimum(s, 0.0) - jnp.log2(1.0 + jnp.exp2(-jnp.abs(s))))
    lf_ref[...] = jnp.maximum(a, c) + jnp.log2(1.0 + jnp.exp2(-jnp.abs(a - c)))
    k_ref[...] = jnp.exp2(c - s).astype(k_ref.dtype)


def _norm_proj(x2, gain, w, lower_bound=None, hgrn_layer=None):
    t, d = x2.shape
    n = w.shape[1]
    tm = TOKEN_TILE
    assert t % tm == 0 and n % d == 0
    row_blk = pl.BlockSpec((tm, d), lambda i: (i, 0))
    in_specs = [row_blk, _resident((1, d)), _resident((d, n))]
    args = [x2, gain.reshape(1, d), w]
    if hgrn_layer is None:
        dtypes = [BF16] * (n // d)
    else:
        assert n == 4 * d
        in_specs.append(_resident(lower_bound.shape))
        args.append(lower_bound)
        dtypes = [BF16, F32, BF16, BF16, BF16]
    return pl.pallas_call(
        functools.partial(_proj_kernel, hgrn_layer=hgrn_layer),
        grid=(t // tm,),
        in_specs=in_specs,
        out_specs=[row_blk] * len(dtypes),
        out_shape=[jax.ShapeDtypeStruct((t, d), dt) for dt in dtypes],
        compiler_params=_params("parallel"),
        name="norm_proj_hgrn" if hgrn_layer is not None else "norm_proj",
    )(*args)


def _mlp_kernel(x_ref, g_ref, w1_ref, w2_ref, *rest, has_proj, final_norm):
    rest = list(rest)
    a_ref, wo_ref = (rest.pop(0), rest.pop(0)) if has_proj else (None, None)
    gf_ref = rest.pop(0) if final_norm else None
    (o_ref,) = rest
    x = x_ref[...]
    if has_proj:
        x = x + _dot(a_ref[...], wo_ref[...])
    h = _rms(x, g_ref[...]).astype(BF16)
    y = x
    for c in range(w1_ref.shape[1] // FF_TILE):
        cols = slice(c * FF_TILE, (c + 1) * FF_TILE)
        a = jnp.maximum(_dot(h, w1_ref[:, cols]), 0.0)
        y = y + _dot((a * a).astype(BF16), w2_ref[cols, :])
    o_ref[...] = _rms(y, gf_ref[...]) if final_norm else y


def _mlp(x2, gain, w1, w2, mixer_out=None, w_out=None, final_gain=None):
    t, d = x2.shape
    tm = TOKEN_TILE
    assert t % tm == 0 and w1.shape[1] % FF_TILE == 0
    has_proj = mixer_out is not None
    final_norm = final_gain is not None
    row_blk = pl.BlockSpec((tm, d), lambda i: (i, 0))
    in_specs = [row_blk, _resident((1, d)), _resident(w1.shape), _resident(w2.shape)]
    args = [x2, gain.reshape(1, d), w1, w2]
    if has_proj:
        in_specs += [row_blk, _resident(w_out.shape)]
        args += [mixer_out, w_out]
    if final_norm:
        in_specs.append(_resident((1, d)))
        args.append(final_gain.reshape(1, d))
    return pl.pallas_call(
        functools.partial(_mlp_kernel, has_proj=has_proj, final_norm=final_norm),
        grid=(t // tm,),
        in_specs=in_specs,
        out_specs=row_blk,
        out_shape=jax.ShapeDtypeStruct((t, d), F32),
        compiler_params=_params("parallel"),
        name="mlp",
    )(*args)


def _gla_levels():
    out, blk = [], A_CHUNK
    while blk >= 2:
        out.append(blk)
        blk //= 2
    return out


def _gla_constants():
    t = np.arange(A_CHUNK)
    sgn, msk = [], []
    for blk in _gla_levels():
        second = (t % blk) >= blk // 2
        sgn.append(np.broadcast_to(np.where(second, 1.0, -1.0)[:, None], (A_CHUNK, A_HEAD)))
        same = (t[:, None] // blk) == (t[None, :] // blk)
        msk.append(same & second[:, None] & (~second)[None, :])
    msk.append(np.eye(A_CHUNK, dtype=bool))
    return jnp.asarray(np.stack(sgn), F32), jnp.asarray(np.stack(msk), F32)


def _bdot_nt(a, b):
    return lax.dot_general(a, b, (((2,), (2,)), ((0,), (0,))), preferred_element_type=F32)


def _gla_kernel(qa_ref, lf_ref, k_ref, v_ref, sg_ref, hn_ref, sgn_ref, msk_ref, o_ref, st_scr, b_scr):
    c_len, grp = A_CHUNK, A_GROUP
    g_len = grp * c_len
    nv = c_len // 8
    seq, dk = qa_ref.shape[1], qa_ref.shape[2]
    levels = _gla_levels()
    sub = lax.broadcasted_iota(jnp.int32, (grp * nv, 8, dk), 1)
    sub8 = lax.broadcasted_iota(jnp.int32, (8, dk), 0)
    st_scr[...] = jnp.zeros_like(st_scr)

    def group(gi, carry):
        r0 = pl.multiple_of(gi * g_len, g_len)
        rows = pl.ds(r0, g_len)
        q = qa_ref[0, rows, :].astype(F32)
        k = k_ref[0, rows, :].astype(F32)
        v3 = v_ref[0, rows, :].reshape(grp, c_len, dk)

        x3 = lf_ref[0, rows, :].reshape(grp * nv, 8, dk)
        for sh in (1, 2, 4):
            x3 = x3 + jnp.where(sub >= sh, pltpu.roll(x3, sh, 1), 0.0)
        tot = x3[:, 7:8, :]
        pieces = []
        for g in range(grp):
            run = None
            for j in range(nv):
                i = g * nv + j
                pieces.append(x3[i] if run is None else x3[i] + run)
                run = tot[i] if run is None else run + tot[i]
        b2 = jnp.concatenate(pieces, axis=0)
        b_scr[rows, :] = b2

        def row_b(r, n):
            return jnp.broadcast_to(b_scr[pl.ds(r0 + r, 1), :], (n, dk))

        b3 = b2.reshape(grp, c_len, dk)

        attn = _bdot_nt(qa_ref[0, rows, :].reshape(grp, c_len, dk),
                        k_ref[0, rows, :].reshape(grp, c_len, dk)) * msk_ref[len(levels)][None]
        for li, blk in enumerate(levels):
            half = blk // 2
            if blk >= 16:
                qk, ex = [], []
                for g in range(grp):
                    for j in range(nv):
                        lo = g * c_len + 8 * j
                        mid = row_b(g * c_len + (8 * j // blk) * blk + half - 1, 8)
                        if (8 * j) % blk >= half:
                            qk.append(q[lo:lo + 8])
                            ex.append(b2[lo:lo + 8] - mid)
                        else:
                            qk.append(k[lo:lo + 8])
                            ex.append(mid - b2[lo:lo + 8])
                x_l = jnp.concatenate(qk, axis=0) * jnp.exp2(jnp.concatenate(ex, axis=0))
                x_l = x_l.astype(BF16).reshape(grp, c_len, dk)
            else:
                parts = []
                for g in range(grp):
                    for j in range(nv):
                        base = g * c_len + 8 * j
                        val = row_b(base + half - 1, 8)
                        for s_i in range(1, 8 // blk):
                            val = jnp.where(sub8 >= s_i * blk, row_b(base + s_i * blk + half - 1, 8), val)
                        parts.append(val)
                b_mid = jnp.concatenate(parts, axis=0).reshape(grp, c_len, dk)
                second = (sub8 % blk >= half)[None]
                qk = jnp.where(second, q.reshape(grp * nv, 8, dk), k.reshape(grp * nv, 8, dk))
                x_l = (qk.reshape(grp, c_len, dk) * jnp.exp2((b3 - b_mid) * sgn_ref[li][None])).astype(BF16)
            attn = attn + _bdot_nt(x_l, x_l) * msk_ref[li][None]

        o_in = lax.dot_general(attn.astype(BF16), v3, (((2,), (1,)), ((0,), (0,))),
                               preferred_element_type=F32)
        b_last = jnp.concatenate([row_b(g * c_len + c_len - 1, c_len) for g in range(grp)], axis=0)
        k_dec = (k * jnp.exp2(b_last - b2)).astype(BF16).reshape(grp, c_len, dk)
        upd = lax.dot_general(v3, k_dec, (((1,), (1,)), ((0,), (0,))), preferred_element_type=F32)
        qe = (q * jnp.exp2(b2)).astype(BF16).reshape(grp, c_len, dk)

        st = st_scr[...]
        sts = []
        for g in range(grp):
            sts.append(st.astype(BF16))
            st = st * jnp.exp2(row_b(g * c_len + c_len - 1, 1)) + upd[g]
        st_scr[...] = st
        o = (o_in + _bdot_nt(qe, jnp.stack(sts, axis=0))).reshape(g_len, dk)
        ms = jnp.mean(o * o, axis=-1, keepdims=True)
        o = o * lax.rsqrt(ms + RMS_EPS) * hn_ref[...]
        o_ref[0, rows, :] = (o * sg_ref[0, rows, :].astype(F32)).astype(o_ref.dtype)
        return carry

    lax.fori_loop(0, seq // g_len, group, 0, unroll=2)


def _gla(qa, lf2, k, v, sg, head_norm):
    bsz, seq, d = qa.shape
    assert seq % (A_GROUP * A_CHUNK) == 0 and d % A_HEAD == 0
    head = pl.BlockSpec((1, seq, A_HEAD), lambda b, h: (b, 0, h))
    sgn, msk = _gla_constants()
    return pl.pallas_call(
        _gla_kernel,
        grid=(bsz, d // A_HEAD),
        in_specs=[head] * 5 + [pl.BlockSpec((1, A_HEAD), lambda b, h: (0, 0)),
                               pl.BlockSpec(sgn.shape, lambda b, h: (0, 0, 0)),
                               pl.BlockSpec(msk.shape, lambda b, h: (0, 0, 0))],
        out_specs=head,
        out_shape=jax.ShapeDtypeStruct((bsz, seq, d), BF16),
        scratch_shapes=[pltpu.VMEM((A_HEAD, A_HEAD), F32), pltpu.VMEM((seq, A_HEAD), F32)],
        compiler_params=_params("parallel", "parallel"),
        name="gla",
    )(qa, lf2, k, v, sg, head_norm.reshape(1, A_HEAD), sgn, msk)


def _t5_large_thresholds():
    max_exact = REL_BUCKETS // 2
    n = np.arange(max_exact, 4 * REL_MAX_DISTANCE, dtype=np.float64)
    large = max_exact + (np.log(n / max_exact) / math.log(REL_MAX_DISTANCE / max_exact)
                         * (REL_BUCKETS - max_exact)).astype(np.int64)
    large = np.minimum(large, REL_BUCKETS - 1)
    assert np.all(np.diff(large) >= 0)
    return [int(n[np.argmax(large >= b)]) for b in range(max_exact + 1, REL_BUCKETS)]


def _moba_kernel(tab_ref, q_ref, k_ref, v_ref, o_ref):
    h = pl.program_id(1)
    seq, dh = q_ref.shape[1], q_ref.shape[2]
    nb = seq // B_BLOCK
    q16 = q_ref[0]
    k16 = k_ref[0]
    v16 = v_ref[0]
    blk_rows = lax.broadcasted_iota(jnp.int32, (dh, seq), 0)
    key_blk = lax.broadcasted_iota(jnp.int32, (dh, seq), 1) // B_BLOCK
    blk_onehot = jnp.where(blk_rows == key_blk, 1.0, 0.0).astype(BF16)

    k_mean = _dot(blk_onehot[:16], k16)[:nb] * (1.0 / B_BLOCK)
    km_hi = k_mean.astype(BF16).astype(F32)
    km_mid = (k_mean - km_hi).astype(BF16).astype(F32)
    km_lo = (k_mean - km_hi - km_mid).astype(BF16).astype(F32)
    km_terms = jnp.concatenate([km_hi, km_mid, km_lo, jnp.zeros_like(km_hi)], axis=0).astype(BF16)
    gate3 = _dot_nt(km_terms, q16)
    gate = gate3[:nb] + gate3[nb:2 * nb] + gate3[2 * nb:3 * nb]
    blk_id = lax.broadcasted_iota(jnp.int32, (nb, seq), 0)
    tok_blk = lax.broadcasted_iota(jnp.int32, (nb, seq), 1) // B_BLOCK
    past = blk_id < tok_blk
    gate = jnp.where(past, gate, -jnp.inf)
    rank = jnp.zeros((nb, seq), jnp.int32)
    for m in range(nb):
        g_m = gate[m:m + 1, :]
        ahead = jnp.where(g_m > gate, 1, jnp.where(jnp.logical_and(g_m == gate, blk_id > m), 1, 0))
        rank = rank + ahead
    shift = jnp.where(past, jnp.where(rank < B_TOPK, 0.0, B_MASK_SHIFT),
                      jnp.where(blk_id == tok_blk, 0.0, B_MASK_SHIFT))
    shift_t = jnp.concatenate([shift, jnp.zeros((dh - nb, seq), F32)], axis=0).T.astype(BF16)
    q_aug = jnp.concatenate([q16, shift_t], axis=1)
    kt_aug = jnp.concatenate([k16.T, blk_onehot], axis=0)
    lane = lax.broadcasted_iota(jnp.int32, (seq, dh), 1)
    v_aug = jnp.concatenate([v16, jnp.where(lane == 0, 1.0, 0.0).astype(BF16)], axis=1)

    width = seq + B_BLOCK
    dist = seq - lax.broadcasted_iota(jnp.int32, (1, width), 1)
    n_pos = jnp.maximum(dist, 0)
    bucket = jnp.minimum(n_pos, REL_BUCKETS // 2)
    for thr in _t5_large_thresholds():
        bucket = bucket + jnp.where(n_pos >= thr, 1, 0)
    rvec = jnp.full((1, width), -jnp.inf, F32)
    for bkt in range(REL_BUCKETS):
        rvec = jnp.where(jnp.logical_and(bucket == bkt, dist >= 0), tab_ref[h, bkt] * LOG2E, rvec)
    tiles = []
    for dl in range(nb):
        a0 = seq - B_BLOCK * dl
        wrow = jnp.concatenate([rvec[:, a0:a0 + B_BLOCK], rvec[:, a0 - B_BLOCK:a0]], axis=1)
        tiles.append(pltpu.roll(jnp.broadcast_to(wrow, (B_BLOCK, 2 * B_BLOCK)), 0, 1,
                                stride=1, stride_axis=0)[:, :B_BLOCK])

    score_scale = LOG2E / math.sqrt(dh)

    def score_block(qt, n):
        q_blk = q_aug[qt * B_BLOCK:(qt + 1) * B_BLOCK]
        return _dot(q_blk, kt_aug[:, n * B_BLOCK:(n + 1) * B_BLOCK]) * score_scale + tiles[qt - n]

    def row_max(blocks):
        mx = None
        for s2 in blocks:
            blk_max = jnp.maximum(s2[:, :dh], s2[:, dh:])
            mx = blk_max if mx is None else jnp.maximum(mx, blk_max)
        return jnp.max(mx, axis=1, keepdims=True)

    def attend(qt, blocks):
        m = row_max(blocks)
        acc = None
        for n, s2 in enumerate(blocks):
            part = _dot(jnp.exp2(s2 - m).astype(BF16), v_aug[n * B_BLOCK:(n + 1) * B_BLOCK])
            acc = part if acc is None else acc + part
        rows = slice(qt * B_BLOCK, (qt + 1) * B_BLOCK)
        o_ref[0, rows, :] = (acc[:, :dh] / acc[:, dh:dh + 1]).astype(o_ref.dtype)

    ready = [[score_block(t, n) for n in range(t + 1)] for t in range(min(B_AHEAD, nb))]
    for qt in range(nb):
        if qt + B_AHEAD < nb:
            ready.append([score_block(qt + B_AHEAD, n) for n in range(qt + B_AHEAD + 1)])
        attend(qt, ready.pop(0))


def _moba(q, k, v, rel_bias):
    bsz, seq, d = q.shape
    assert seq % B_BLOCK == 0 and seq // B_BLOCK <= 128 and d % B_HEAD == 0
    head = pl.BlockSpec((1, seq, B_HEAD), lambda b, h: (b, 0, h))
    return pl.pallas_call(
        _moba_kernel,
        grid=(bsz, d // B_HEAD),
        in_specs=[pl.BlockSpec(memory_space=pltpu.SMEM), head, head, head],
        out_specs=head,
        out_shape=jax.ShapeDtypeStruct((bsz, seq, d), BF16),
        compiler_params=_params("parallel", "parallel"),
        name="moba",
    )(rel_bias.T.astype(F32), q, k, v)


def _pool_kernel(x_ref, xp_ref, g_ref, w_ref, sc_ref, o_ref):
    i = pl.program_id(1)
    ts = x_ref.shape[1]
    x = x_ref[0]
    gain = g_ref[...]
    h = _rms(x, gain)
    h_prev = jnp.where(i > 0, _rms(xp_ref[0], gain), 0.0)
    hc = jnp.concatenate([h_prev, h], axis=0)
    pos1 = (i * ts + 1 + lax.broadcasted_iota(jnp.int32, (ts, 1), 0)).astype(F32)
    cg = x.shape[1] // len(C_WINDOWS)
    outs = []
    for g, w in enumerate(C_WINDOWS):
        cols = slice(g * cg, (g + 1) * cg)
        acc = hc[:, cols]
        span = 1
        while span < w:
            acc = acc + pltpu.roll(acc, span, 0)
            span *= 2
        pooled = acc[C_HALO:] / jnp.minimum(pos1, float(w))
        outs.append(_dot((pooled - h[:, cols]).astype(BF16), w_ref[g]))
    y = jnp.concatenate(outs, axis=1) * sc_ref[...]
    o_ref[0] = x + y


def _pool(x3, gain, w_group, scale):
    bsz, seq, d = x3.shape
    ts = POOL_TILE
    assert seq % ts == 0 and ts % C_HALO == 0 and max(C_WINDOWS) <= C_HALO
    per = ts // C_HALO
    return pl.pallas_call(
        _pool_kernel,
        grid=(bsz, seq // ts),
        in_specs=[
            pl.BlockSpec((1, ts, d), lambda b, i: (b, i, 0)),
            pl.BlockSpec((1, C_HALO, d), lambda b, i: (b, jnp.maximum(i * per - 1, 0), 0)),
            pl.BlockSpec((1, d), lambda b, i: (0, 0)),
            pl.BlockSpec(w_group.shape, lambda b, i: (0, 0, 0)),
            pl.BlockSpec((1, d), lambda b, i: (0, 0)),
        ],
        out_specs=pl.BlockSpec((1, ts, d), lambda b, i: (b, i, 0)),
        out_shape=jax.ShapeDtypeStruct((bsz, seq, d), F32),
        compiler_params=_params("parallel", "parallel"),
        name="pool",
    )(x3, x3, gain.reshape(1, d), w_group, scale.reshape(1, d))


def kernel(x, norm_mix, norm_mlp, norm_final, a_w_in, a_lower_bound, a_head_norm, a_w_out,
           b_w_qkv, b_w_out, rel_bias, c_w_group, c_scale, mlp_w1, mlp_w2):
    bsz, seq, d = x.shape
    depth = norm_mix.shape[0]
    t = bsz * seq
    x2 = x.reshape(t, d)
    for i in range(depth):
        mixer, j = i % N_MIXERS, i // N_MIXERS
        mixer_out = w_out = None
        if mixer == 0:
            parts = _norm_proj(x2, norm_mix[i], a_w_in[j].astype(BF16),
                               lower_bound=a_lower_bound.astype(F32), hgrn_layer=j)
            mixer_out = _gla(*[p.reshape(bsz, seq, d) for p in parts], a_head_norm[j]).reshape(t, d)
            w_out = a_w_out[j].astype(BF16)
        elif mixer == 1:
            qkv = _norm_proj(x2, norm_mix[i], b_w_qkv[j].astype(BF16))
            mixer_out = _moba(*[p.reshape(bsz, seq, d) for p in qkv], rel_bias).reshape(t, d)
            w_out = b_w_out[j].astype(BF16)
        else:
            x2 = _pool(x2.reshape(bsz, seq, d), norm_mix[i], c_w_group[j].astype(BF16),
                       c_scale[j]).reshape(t, d)
        x2 = _mlp(x2, norm_mlp[i], mlp_w1[i].astype(BF16), mlp_w2[i].astype(BF16),
                  mixer_out=mixer_out, w_out=w_out,
                  final_gain=norm_final if i == depth - 1 else None)
    return x2.reshape(bsz, seq, d)
```
